```python
import math
import jax, jax.numpy as jnp
from jax import lax
import numpy as np

D_MODEL = 1024
BATCH = 8
SEQ = 4096
DEPTH = 2

CTX_LEN = 256
GRID_W = 64
N_EVEN = (DEPTH + 1) // 2
N_ODD = DEPTH // 2
EPS = 1e-6

HY_WIDTH = D_MODEL // 2
HY_SHORT_K = 3
HY_EMB_DIM = 33
HY_FILTER_HIDDEN = 64
HY_DECAY_FAST = 0.3
HY_DECAY_SLOW = 1.5
HY_DECAY_TARGET = 1e-2

NA_HEADS = 8
NA_HEAD_DIM = 64
NA_WIDTH = NA_HEADS * NA_HEAD_DIM
NA_WIN_ROWS = 8
NA_WIN_COLS = 16
NA_QBLOCK_COLS = 16
NA_KBLOCK_COLS = 32

CONF_WIDTH = D_MODEL
CONF_K = 31

D_FF = 2816
FFN_CONV_K = 3

NEG_BIG = -1e30

kernel_name = 'hybrid_hyena_natten_conformer_dit'


def rms_norm(t, g):
    tf = t.astype(jnp.float32)
    y = tf * lax.rsqrt(jnp.mean(tf * tf, axis=-1, keepdims=True) + EPS)
    return (y * g.astype(jnp.float32)).astype(t.dtype)


def layer_norm(t, g, b):
    tf = t.astype(jnp.float32)
    mu = jnp.mean(tf, axis=-1, keepdims=True)
    var = jnp.mean(jnp.square(tf - mu), axis=-1, keepdims=True)
    y = (tf - mu) * lax.rsqrt(var + EPS)
    return (y * g.astype(jnp.float32) + b.astype(jnp.float32)).astype(t.dtype)


def dwconv(t, w, b):
    k, ch = w.shape
    y = lax.conv_general_dilated(
        t, w[:, None, :].astype(t.dtype), window_strides=(1,),
        padding=[(k // 2, k // 2)], dimension_numbers=('NWC', 'WIO', 'NWC'),
        feature_group_count=ch)
    return y + b.astype(t.dtype)


def adaln_params(cond, w_mod, b_mod):
    m = jax.nn.silu(cond) @ w_mod + b_mod
    return jnp.split(m[:, None, :], 6, axis=-1)


def hyena_filters(length, w1, b1, w2, b2, w3, b3, w4, freq):
    bands = (HY_EMB_DIM - 1) // 2
    t01 = jnp.linspace(0.0, 1.0, length, dtype=jnp.float32)[:, None]
    w_pos = 2.0 * math.pi * jnp.arange(length, dtype=jnp.float32) / length
    f = jnp.linspace(1e-4, bands - 1, bands, dtype=jnp.float32)
    ang = w_pos[:, None] * f[None, :]
    z = jnp.concatenate([t01, jnp.cos(ang), -jnp.sin(ang)], axis=-1).astype(w1.dtype)
    hdn = jnp.sin(freq[0] * (z @ w1 + b1))
    hdn = jnp.sin(freq[1] * (hdn @ w2 + b2))
    hdn = jnp.sin(freq[2] * (hdn @ w3 + b3))
    filt = (hdn @ w4).astype(jnp.float32)
    deltas = jnp.abs(jnp.linspace(math.log(HY_DECAY_TARGET) / HY_DECAY_FAST,
                                  math.log(HY_DECAY_TARGET) / HY_DECAY_SLOW,
                                  HY_WIDTH, dtype=jnp.float32))
    decay = jnp.exp(-t01 * deltas[None, :])
    return filt[:, :HY_WIDTH] * decay, filt[:, HY_WIDTH:] * decay


def bidirectional_fft_conv(v, h_fwd, h_bwd, bias):
    length = v.shape[1]
    n = 2 * length
    k_full = jnp.concatenate([h_fwd, jnp.zeros_like(h_fwd[:1]), h_bwd[:0:-1]], axis=0)
    v_f = jnp.fft.rfft(v.astype(jnp.float32), n=n, axis=1)
    k_f = jnp.fft.rfft(k_full, n=n, axis=0)
    y = jnp.fft.irfft(v_f * k_f[None], n=n, axis=1)[:, :length]
    return (y + v.astype(jnp.float32) * bias.astype(jnp.float32)).astype(v.dtype)


def hyena_mixer(u, short_w, short_b, w1, b1, w2, b2, w3, b3, w4, freq, bias):
    uc = dwconv(u, short_w, short_b)
    x0, x1, v = jnp.split(uc, 3, axis=-1)
    h_fwd, h_bwd = hyena_filters(u.shape[1], w1, b1, w2, b2, w3, b3, w4, freq)
    return x0 * bidirectional_fft_conv(x1 * v, h_fwd, h_bwd, bias)


def neighbourhood_attention(q, k, v, k_ctx, v_ctx, rpb):
    bsz, seq, _ = q.shape
    rows = seq // GRID_W
    wr = min(NA_WIN_ROWS, rows)
    n_cb = GRID_W // NA_QBLOCK_COLS
    scale = NA_HEAD_DIM ** -0.5
    grid = lambda t: t.reshape(bsz, rows, GRID_W, NA_HEADS, NA_HEAD_DIM)
    qg, kg, vg = grid(q), grid(k), grid(v)
    kc = k_ctx.reshape(bsz, -1, NA_HEADS, NA_HEAD_DIM)
    vc = v_ctx.reshape(bsz, -1, NA_HEADS, NA_HEAD_DIM)
    qcol = np.arange(GRID_W).reshape(n_cb, NA_QBLOCK_COLS)
    win_start = np.clip(qcol - NA_WIN_COLS // 2, 0, GRID_W - NA_WIN_COLS)
    kb_start = np.clip(np.arange(n_cb) * NA_QBLOCK_COLS - NA_WIN_COLS // 2, 0, GRID_W - NA_KBLOCK_COLS)
    kcol = kb_start[:, None] + np.arange(NA_KBLOCK_COLS)[None, :]
    col_ok = (kcol[:, None, :] >= win_start[:, :, None]) & (kcol[:, None, :] < win_start[:, :, None] + NA_WIN_COLS)
    dc_idx = np.clip(kcol[:, None, :] - qcol[:, :, None] + NA_WIN_COLS - 1, 0, 2 * NA_WIN_COLS - 2)
    col_ok = jnp.asarray(col_ok)
    dc_idx = jnp.asarray(dc_idx)
    n_loc = wr * NA_KBLOCK_COLS

    def one_row(r):
        rs = jnp.clip(r - wr // 2, 0, rows - wr)
        q_r = lax.dynamic_index_in_dim(qg, r, axis=1, keepdims=False)
        q_blk = (q_r * scale).reshape(bsz, n_cb, NA_QBLOCK_COLS, NA_HEADS, NA_HEAD_DIM)
        k_band = lax.dynamic_slice_in_dim(kg, rs, wr, axis=1)
        v_band = lax.dynamic_slice_in_dim(vg, rs, wr, axis=1)
        k_blk = k_band[:, :, kcol]
        v_blk = v_band[:, :, kcol]
        s_loc = jnp.einsum('bjqhd,brjkhd->bhjqrk', q_blk, k_blk).astype(jnp.float32)
        dr_idx = rs + jnp.arange(wr) - r + NA_WIN_ROWS - 1
        bias = rpb[:, dr_idx][:, :, dc_idx].astype(jnp.float32)
        s_loc = s_loc + bias.transpose(0, 2, 3, 1, 4)
        s_loc = jnp.where(col_ok[:, :, None, :], s_loc, NEG_BIG)
        s_loc = s_loc.reshape(bsz, NA_HEADS, n_cb, NA_QBLOCK_COLS, n_loc)
        s_ctx = jnp.einsum('bjqhd,bchd->bhjqc', q_blk, kc).astype(jnp.float32)
        p = jax.nn.softmax(jnp.concatenate([s_loc, s_ctx], axis=-1), axis=-1).astype(v.dtype)
        p_loc = p[..., :n_loc].reshape(bsz, NA_HEADS, n_cb, NA_QBLOCK_COLS, wr, NA_KBLOCK_COLS)
        p_ctx = p[..., n_loc:]
        o = (jnp.einsum('bhjqrk,brjkhd->bjqhd', p_loc, v_blk)
             + jnp.einsum('bhjqc,bchd->bjqhd', p_ctx, vc))
        return o.reshape(bsz, GRID_W, NA_WIDTH)

    out = lax.map(one_row, jnp.arange(rows))
    return out.transpose(1, 0, 2, 3).reshape(bsz, seq, NA_WIDTH)


def conformer_conv(t, w_pw1, b_pw1, w_dw, b_dw, ln_g, ln_b, w_pw2, b_pw2):
    a, g = jnp.split(t @ w_pw1 + b_pw1, 2, axis=-1)
    u = a * jax.nn.sigmoid(g)
    u = dwconv(u, w_dw, b_dw)
    u = jax.nn.silu(layer_norm(u, ln_g, ln_b))
    return u @ w_pw2 + b_pw2


def conv_ffn(t, w_up, w_dw, b_dw, w_down):
    z = dwconv(t @ w_up, w_dw, b_dw)
    g, u = jnp.split(z, 2, axis=-1)
    return (jax.nn.silu(g) * u) @ w_down


def setup_inputs(seed: int = 0) -> dict:
    key = jax.random.key(seed)
    ks = iter(jax.random.split(key, 48))
    D = D_MODEL
    nz = 3 * HY_WIDTH + 3 * NA_WIDTH

    def nrm(shape, scale):
        return scale * jax.random.normal(next(ks), shape, jnp.float32)

    def gain(shape):
        return 1.0 + nrm(shape, 0.1)

    return {
        'x': nrm((BATCH, SEQ, D), 1.0),
        'c': nrm((BATCH, D), 1.0),
        'ctx': nrm((BATCH, CTX_LEN, D), 1.0),
        'c_ctx': nrm((D,), 1.0),
        'w_mod': nrm((DEPTH, D, 6 * D), 0.5 * D ** -0.5),
        'b_mod': nrm((DEPTH, 6 * D), 0.02),
        'g_mix_pre': gain((DEPTH, D)),
        'g_mix_post': gain((DEPTH, D)),
        'g_ffn_pre': gain((DEPTH, D)),
        'g_ffn_post': gain((DEPTH, D)),
        'w_in': nrm((N_EVEN, D, nz), D ** -0.5),
        'w_out': nrm((N_EVEN, HY_WIDTH + NA_WIDTH, D), (HY_WIDTH + NA_WIDTH) ** -0.5),
        'hy_short_w': nrm((N_EVEN, HY_SHORT_K, 3 * HY_WIDTH), HY_SHORT_K ** -0.5),
        'hy_short_b': nrm((N_EVEN, 3 * HY_WIDTH), 0.02),
        'hy_f_w1': nrm((N_EVEN, HY_EMB_DIM, HY_FILTER_HIDDEN), HY_EMB_DIM ** -0.5),
        'hy_f_b1': nrm((N_EVEN, HY_FILTER_HIDDEN), 0.02),
        'hy_f_w2': nrm((N_EVEN, HY_FILTER_HIDDEN, HY_FILTER_HIDDEN), HY_FILTER_HIDDEN ** -0.5),
        'hy_f_b2': nrm((N_EVEN, HY_FILTER_HIDDEN), 0.02),
        'hy_f_w3': nrm((N_EVEN, HY_FILTER_HIDDEN, HY_FILTER_HIDDEN), HY_FILTER_HIDDEN ** -0.5),
        'hy_f_b3': nrm((N_EVEN, HY_FILTER_HIDDEN), 0.02),
        'hy_f_w4': nrm((N_EVEN, HY_FILTER_HIDDEN, 2 * HY_WIDTH), HY_FILTER_HIDDEN ** -0.5),
        'hy_f_freq': gain((N_EVEN, 3, HY_FILTER_HIDDEN)),
        'hy_bias': nrm((N_EVEN, HY_WIDTH), 0.5),
        'na_rpb': nrm((N_EVEN, NA_HEADS, 2 * NA_WIN_ROWS - 1, 2 * NA_WIN_COLS - 1), 0.1),
        'cf_w_pw1': nrm((N_ODD, D, 2 * CONF_WIDTH), D ** -0.5),
        'cf_b_pw1': nrm((N_ODD, 2 * CONF_WIDTH), 0.02),
        'cf_w_dw': nrm((N_ODD, CONF_K, CONF_WIDTH), CONF_K ** -0.5),
        'cf_b_dw': nrm((N_ODD, CONF_WIDTH), 0.02),
        'cf_ln_g': gain((N_ODD, CONF_WIDTH)),
        'cf_ln_b': nrm((N_ODD, CONF_WIDTH), 0.02),
        'cf_w_pw2': nrm((N_ODD, CONF_WIDTH, D), CONF_WIDTH ** -0.5),
        'cf_b_pw2': nrm((N_ODD, D), 0.02),
        'ffn_w_up': nrm((DEPTH, D, 2 * D_FF), D ** -0.5),
        'ffn_w_dw': nrm((DEPTH, FFN_CONV_K, 2 * D_FF), FFN_CONV_K ** -0.5),
        'ffn_b_dw': nrm((DEPTH, 2 * D_FF), 0.02),
        'ffn_w_down': nrm((DEPTH, D_FF, D), D_FF ** -0.5),
    }


def reference(x, c, ctx, c_ctx, w_mod, b_mod, g_mix_pre, g_mix_post, g_ffn_pre, g_ffn_post,
              w_in, w_out, hy_short_w, hy_short_b, hy_f_w1, hy_f_b1, hy_f_w2, hy_f_b2,
              hy_f_w3, hy_f_b3, hy_f_w4, hy_f_freq, hy_bias, na_rpb,
              cf_w_pw1, cf_b_pw1, cf_w_dw, cf_b_dw, cf_ln_g, cf_ln_b, cf_w_pw2, cf_b_pw2,
              ffn_w_up, ffn_w_dw, ffn_b_dw, ffn_w_down):
    h = x
    ctx_stream = ctx
    for layer in range(DEPTH):
        shift1, scale1, gate1, shift2, scale2, gate2 = adaln_params(c, w_mod[layer], b_mod[layer])
        hn = rms_norm(h, g_mix_pre[layer]) * (1 + scale1) + shift1
        if layer % 2 == 0:
            e = layer // 2
            m_ctx = jax.nn.silu(c_ctx) @ w_mod[layer][:, :2 * D_MODEL] + b_mod[layer][:2 * D_MODEL]
            cshift, cscale = jnp.split(m_ctx, 2)
            cn = rms_norm(ctx_stream, g_mix_pre[layer]) * (1 + cscale) + cshift
            k_ctx, v_ctx = jnp.split(cn @ w_in[e][:, 3 * HY_WIDTH + NA_WIDTH:], 2, axis=-1)
            z = hn @ w_in[e]
            y_hy = hyena_mixer(z[..., :3 * HY_WIDTH], hy_short_w[e], hy_short_b[e],
                               hy_f_w1[e], hy_f_b1[e], hy_f_w2[e], hy_f_b2[e],
                               hy_f_w3[e], hy_f_b3[e], hy_f_w4[e], hy_f_freq[e], hy_bias[e])
            q, k, v = jnp.split(z[..., 3 * HY_WIDTH:], 3, axis=-1)
            y_na = neighbourhood_attention(q, k, v, k_ctx, v_ctx, na_rpb[e])
            y = jnp.concatenate([y_hy, y_na], axis=-1) @ w_out[e]
        else:
            o = layer // 2
            y = conformer_conv(hn, cf_w_pw1[o], cf_b_pw1[o], cf_w_dw[o], cf_b_dw[o],
                               cf_ln_g[o], cf_ln_b[o], cf_w_pw2[o], cf_b_pw2[o])
        h = h + gate1 * rms_norm(y, g_mix_post[layer])
        hn = rms_norm(h, g_ffn_pre[layer]) * (1 + scale2) + shift2
        y = conv_ffn(hn, ffn_w_up[layer], ffn_w_dw[layer], ffn_b_dw[layer], ffn_w_down[layer])
        h = h + gate2 * rms_norm(y, g_ffn_post[layer])
    return h
```

```python
import functools
import math

import numpy as np
import jax
import jax.numpy as jnp
from jax import lax
from jax.experimental import pallas as pl
from jax.experimental.pallas import tpu as pltpu

F32 = jnp.float32
BF16 = jnp.bfloat16

EPS = 1e-6
NEG_BIG = -1e30

GRID_W = 64
NA_HEADS = 8
NA_HEAD_DIM = 64
NA_WIN_ROWS = 8
NA_WIN_COLS = 16
HY_SHORT_K = 3
HY_EMB_DIM = 33
HY_DECAY_FAST = 0.3
HY_DECAY_SLOW = 1.5
HY_DECAY_TARGET = 1e-2
CONF_K = 31
FFN_CONV_K = 3

LANES = 128
BF16_SUBLANES = 16
VMEM_LIMIT = 56 * 1024 * 1024

HALO = BF16_SUBLANES
DFT_N2 = 64
FFN_CHUNK = 256
EMB_PAD = LANES


def _cparams(n_axes):
    return pltpu.CompilerParams(dimension_semantics=("arbitrary",) * n_axes,
                                vmem_limit_bytes=VMEM_LIMIT)


def _const_spec(shape):
    nd = len(shape)
    return pl.BlockSpec(shape, lambda *_: (0,) * nd, pipeline_mode=pl.Buffered(1))


def _silu(t):
    return t * jax.nn.sigmoid(t)


def _mod_norm(t, g, scale, shift):
    ms = jnp.mean(t * t, axis=-1, keepdims=True)
    return (t * lax.rsqrt(ms + EPS) * g) * (1.0 + scale) + shift


def _rms(t, g):
    ms = jnp.mean(t * t, axis=-1, keepdims=True)
    return t * lax.rsqrt(ms + EPS) * g


def _dot(a, b):
    return jnp.dot(a, b, preferred_element_type=F32)


def _dot_hi(a, b):
    return jnp.dot(a, b, preferred_element_type=F32, precision=lax.Precision.HIGHEST)


def _fill_halo_tile(hn_ref, prev_ref, main_ref, next_ref, norm_fn, tile):
    i = pl.program_id(1)
    last = pl.num_programs(1) - 1
    p = norm_fn(prev_ref[0])
    n = norm_fn(next_ref[0])
    hn_ref[0:HALO, :] = jnp.where(i > 0, p, 0.0).astype(BF16)
    hn_ref[HALO:HALO + tile, :] = norm_fn(main_ref[0]).astype(BF16)
    hn_ref[HALO + tile:, :] = jnp.where(i < last, n, 0.0).astype(BF16)


def _halo_specs(tile, d, seq):
    r = tile // HALO
    nblk = seq // HALO
    return [
        pl.BlockSpec((1, HALO, d), lambda b, i: (b, jnp.maximum(i * r - 1, 0), 0)),
        pl.BlockSpec((1, tile, d), lambda b, i: (b, i, 0)),
        pl.BlockSpec((1, HALO, d), lambda b, i: (b, jnp.minimum((i + 1) * r, nblk - 1), 0)),
    ]


def _adaln_kernel(c_ref, w_ref, b_ref, o_ref):
    s = _silu(c_ref[...]).astype(BF16)
    o_ref[0] = _dot(s, w_ref[0].astype(BF16)) + b_ref[0]


def _adaln(cc, w_mod, b_mod):
    depth, d, n6 = w_mod.shape
    r = cc.shape[0]
    nb = d
    return pl.pallas_call(
        _adaln_kernel,
        grid=(depth, n6 // nb),
        in_specs=[pl.BlockSpec((r, d), lambda l, j: (0, 0)),
                  pl.BlockSpec((1, d, nb), lambda l, j: (l, 0, j)),
                  pl.BlockSpec((1, 1, nb), lambda l, j: (l, 0, j))],
        out_specs=pl.BlockSpec((1, r, nb), lambda l, j: (l, 0, j)),
        out_shape=jax.ShapeDtypeStruct((depth, r, n6), F32),
        compiler_params=_cparams(2),
        name="adaln",
    )(cc, w_mod, b_mod.reshape(depth, 1, n6))


def _inproj_kernel(prev_ref, main_ref, next_ref, mod_ref, g_ref, why_ref, wqkv_ref, sw_ref, sb_ref,
                   x0_ref, xv_ref, q_ref, k_ref, v_ref, hn_ref, z_ref, *, tile, d, hw, aw):
    shift = mod_ref[0, :, 0:d]
    scale = mod_ref[0, :, d:2 * d]
    g = g_ref[...]
    _fill_halo_tile(hn_ref, prev_ref, main_ref, next_ref,
                    lambda t: _mod_norm(t, g, scale, shift), tile)
    z_ref[...] = _dot(hn_ref[...], why_ref[...])
    uc = (sw_ref[0:1, :] * z_ref[pl.ds(HALO - 1, tile), :]
          + sw_ref[1:2, :] * z_ref[pl.ds(HALO, tile), :]
          + sw_ref[2:3, :] * z_ref[pl.ds(HALO + 1, tile), :]
          + sb_ref[...])
    x0_ref[0] = uc[:, 0:hw]
    xv_ref[0] = uc[:, hw:2 * hw] * uc[:, 2 * hw:3 * hw]
    qkv = _dot(hn_ref[pl.ds(HALO, tile), :], wqkv_ref[...])
    q_ref[0] = (qkv[:, 0:aw] * (NA_HEAD_DIM ** -0.5)).astype(BF16)
    k_ref[0] = qkv[:, aw:2 * aw].astype(BF16)
    v_ref[0] = qkv[:, 2 * aw:3 * aw].astype(BF16)


def _inproj(h, mod, g, w_hy, w_qkv, short_w, short_b, tile):
    b, s, d = h.shape
    hw = w_hy.shape[1] // 3
    aw = w_qkv.shape[1] // 3
    tok = lambda width: pl.BlockSpec((1, tile, width), lambda bb, i: (bb, i, 0))
    return pl.pallas_call(
        functools.partial(_inproj_kernel, tile=tile, d=d, hw=hw, aw=aw),
        grid=(b, s // tile),
        in_specs=_halo_specs(tile, d, s) + [
            pl.BlockSpec((1, 1, mod.shape[2]), lambda bb, i: (bb, 0, 0)),
            _const_spec((1, d)),
            _const_spec(w_hy.shape),
            _const_spec(w_qkv.shape),
            _const_spec(short_w.shape),
            _const_spec((1, 3 * hw)),
        ],
        out_specs=[tok(hw), tok(hw), tok(aw), tok(aw), tok(aw)],
        out_shape=[jax.ShapeDtypeStruct((b, s, hw), F32), jax.ShapeDtypeStruct((b, s, hw), F32),
                   jax.ShapeDtypeStruct((b, s, aw), BF16), jax.ShapeDtypeStruct((b, s, aw), BF16),
                   jax.ShapeDtypeStruct((b, s, aw), BF16)],
        scratch_shapes=[pltpu.VMEM((tile + 2 * HALO, d), BF16),
                        pltpu.VMEM((tile + 2 * HALO, 3 * hw), F32)],
        compiler_params=_cparams(2),
        name="inproj",
    )(h, h, h, mod, g.reshape(1, d), w_hy, w_qkv, short_w, short_b.reshape(1, 3 * hw))


def _ctxkv_kernel(ctx_ref, mod_ref, g_ref, w_ref, k_ref, v_ref, *, d, aw):
    shift = mod_ref[:, 0:d]
    scale = mod_ref[:, d:2 * d]
    cn = _mod_norm(ctx_ref[0], g_ref[...], scale, shift).astype(BF16)
    kv = _dot(cn, w_ref[...])
    k_ref[0] = kv[:, 0:aw].astype(BF16)
    v_ref[0] = kv[:, aw:2 * aw].astype(BF16)


def _ctxkv(ctx, mod_ctx, g, w_kv):
    b, n, d = ctx.shape
    aw = w_kv.shape[1] // 2
    out = pl.BlockSpec((1, n, aw), lambda bb: (bb, 0, 0))
    return pl.pallas_call(
        functools.partial(_ctxkv_kernel, d=d, aw=aw),
        grid=(b,),
        in_specs=[pl.BlockSpec((1, n, d), lambda bb: (bb, 0, 0)),
                  _const_spec(mod_ctx.shape), _const_spec((1, d)), _const_spec(w_kv.shape)],
        out_specs=[out, out],
        out_shape=[jax.ShapeDtypeStruct((b, n, aw), BF16)] * 2,
        compiler_params=_cparams(1),
        name="ctxkv",
    )(ctx, mod_ctx, g.reshape(1, d), w_kv)


def _filter_kernel(z_ref, w1_ref, b1_ref, w2_ref, b2_ref, w3_ref, b3_ref, w4_ref, fr_ref, dl_ref,
                   hf_ref, hb_ref, *, rows, hw):
    z = z_ref[...]
    hdn = jnp.sin(fr_ref[0:1, :] * (_dot_hi(z, w1_ref[...]) + b1_ref[...]))
    hdn = jnp.sin(fr_ref[1:2, :] * (_dot_hi(hdn, w2_ref[...]) + b2_ref[...]))
    hdn = jnp.sin(fr_ref[2:3, :] * (_dot_hi(hdn, w3_ref[...]) + b3_ref[...]))
    filt = _dot_hi(hdn, w4_ref[...])
    decay = jnp.exp(-z[:, 0:1] * dl_ref[...])
    hf_ref[...] = filt[:, 0:hw] * decay
    pos = pl.program_id(0) * rows + lax.broadcasted_iota(jnp.int32, (rows, 1), 0)
    hb_ref[...] = jnp.where(pos > 0, filt[:, hw:2 * hw] * decay, 0.0)


def _hyena_filters(length, w1, b1, w2, b2, w3, b3, w4, freq):
    hid = w1.shape[1]
    hw = w4.shape[1] // 2
    bands = (HY_EMB_DIM - 1) // 2
    t01 = jnp.linspace(0.0, 1.0, length, dtype=F32)[:, None]
    w_pos = 2.0 * math.pi * jnp.arange(length, dtype=F32) / length
    f = jnp.linspace(1e-4, bands - 1, bands, dtype=F32)
    ang = w_pos[:, None] * f[None, :]
    z = jnp.concatenate([t01, jnp.cos(ang), -jnp.sin(ang),
                         jnp.zeros((length, EMB_PAD - HY_EMB_DIM), F32)], axis=-1)
    w1p = jnp.concatenate([w1, jnp.zeros((EMB_PAD - HY_EMB_DIM, hid), F32)], axis=0)
    deltas = jnp.abs(jnp.linspace(math.log(HY_DECAY_TARGET) / HY_DECAY_FAST,
                                  math.log(HY_DECAY_TARGET) / HY_DECAY_SLOW, hw, dtype=F32))[None, :]
    rows = min(length, 1024)
    out = pl.BlockSpec((rows, hw), lambda i: (i, 0))
    return pl.pallas_call(
        functools.partial(_filter_kernel, rows=rows, hw=hw),
        grid=(length // rows,),
        in_specs=[pl.BlockSpec((rows, EMB_PAD), lambda i: (i, 0)),
                  _const_spec(w1p.shape), _const_spec((1, hid)),
                  _const_spec(w2.shape), _const_spec((1, hid)),
                  _const_spec(w3.shape), _const_spec((1, hid)),
                  _const_spec(w4.shape), _const_spec(freq.shape), _const_spec(deltas.shape)],
        out_specs=[out, out],
        out_shape=[jax.ShapeDtypeStruct((length, hw), F32)] * 2,
        compiler_params=_cparams(1),
        name="hyena_filter",
    )(z, w1p, b1.reshape(1, hid), w2, b2.reshape(1, hid), w3, b3.reshape(1, hid), w4, freq, deltas)


def _dft_tables(seq):
    n2 = DFT_N2
    h1 = seq // n2
    n1 = 2 * h1
    n = n1 * n2
    k1 = np.arange(n1)[:, None]
    t1 = np.arange(h1)[None, :]
    a1 = 2.0 * np.pi * ((k1 * t1) % n1) / n1
    f1 = np.concatenate([np.cos(a1), -np.sin(a1)], axis=0)
    f3 = np.concatenate([np.cos(a1).T, -np.sin(a1).T], axis=1) / n
    kk1 = np.arange(n1)[:, None, None]
    k2 = np.arange(n2)[None, :, None]
    t2 = np.arange(n2)[None, None, :]
    th = 2.0 * np.pi * ((t2 * k2 * n1 + t2 * kk1) % n) / n
    wr, wi = np.cos(th), -np.sin(th)
    m2f = np.concatenate([np.concatenate([wr, -wi], axis=2),
                          np.concatenate([wi, wr], axis=2)], axis=1)
    m2i = np.transpose(m2f, (0, 2, 1))
    as_bf16 = lambda a: jnp.asarray(a, F32).astype(BF16)
    return as_bf16(f1), as_bf16(m2f), as_bf16(m2i), as_bf16(f3)


def _dft_stage1(load_rows, f1_ref, a_ref, n1):
    def body(t2, carry):
        p = _dot(f1_ref[...], load_rows(t2).astype(BF16))
        a_ref[pl.ds(pl.multiple_of(t2 * 2 * n1, 2 * n1), 2 * n1), :] = p
        return carry
    lax.fori_loop(0, DFT_N2, body, 0)


def _dft_stage2(a_ref, m2f_ref, k1, n1):
    ar = a_ref[pl.ds(k1, DFT_N2, stride=2 * n1), :]
    ai = a_ref[pl.ds(n1 + k1, DFT_N2, stride=2 * n1), :]
    return _dot(m2f_ref[k1], jnp.concatenate([ar, ai], axis=0).astype(BF16))


def _spectrum_kernel(hf_ref, hb_ref, f1_ref, m2f_ref, kf_ref, af_ref, ab_ref, *, n1):
    n2 = DFT_N2
    _dft_stage1(lambda t2: hf_ref[pl.ds(t2, n1 // 2, stride=n2), :], f1_ref, af_ref, n1)
    _dft_stage1(lambda t2: hb_ref[pl.ds(t2, n1 // 2, stride=n2), :], f1_ref, ab_ref, n1)

    def body(k1, carry):
        xf = _dft_stage2(af_ref, m2f_ref, k1, n1)
        xb = _dft_stage2(ab_ref, m2f_ref, k1, n1)
        sign = jnp.where(lax.broadcasted_iota(jnp.int32, (2 * n2, 1), 0) < n2, 1.0, -1.0)
        kf_ref[k1] = xf + sign * xb
        return carry
    lax.fori_loop(0, n1, body, 0)


def _filter_spectrum(hf, hb, f1, m2f):
    s, hw = hf.shape
    n2 = DFT_N2
    n1 = 2 * s // n2
    cb = LANES
    col = pl.BlockSpec((s, cb), lambda j: (0, j))
    return pl.pallas_call(
        functools.partial(_spectrum_kernel, n1=n1),
        grid=(hw // cb,),
        in_specs=[col, col, _const_spec(f1.shape), _const_spec(m2f.shape)],
        out_specs=pl.BlockSpec((n1, 2 * n2, cb), lambda j: (0, 0, j)),
        out_shape=jax.ShapeDtypeStruct((n1, 2 * n2, hw), F32),
        scratch_shapes=[pltpu.VMEM((n2 * 2 * n1, cb), F32), pltpu.VMEM((n2 * 2 * n1, cb), F32)],
        compiler_params=_cparams(1),
        name="filter_spectrum",
    )(hf, hb, f1, m2f)


def _longconv_kernel(x0_ref, xv_ref, kf_ref, bias_ref, f1_ref, m2f_ref, m2i_ref, f3_ref, o_ref,
                     a_ref, b_ref, *, n1):
    n2 = DFT_N2
    h1 = n1 // 2
    _dft_stage1(lambda t2: xv_ref[0, pl.ds(t2, h1, stride=n2), :], f1_ref, a_ref, n1)

    def per_k1(k1, carry):
        x = _dft_stage2(a_ref, m2f_ref, k1, n1)
        kf = kf_ref[k1]
        xr, xi = x[0:n2], x[n2:2 * n2]
        kr, ki = kf[0:n2], kf[n2:2 * n2]
        y = jnp.concatenate([xr * kr - xi * ki, xr * ki + xi * kr], axis=0).astype(BF16)
        b_ref[pl.ds(pl.multiple_of(k1 * 2 * n2, 2 * n2), 2 * n2), :] = _dot(m2i_ref[k1], y)
        return carry
    lax.fori_loop(0, n1, per_k1, 0)

    def per_t2(t2, carry):
        br = b_ref[pl.ds(t2, n1, stride=2 * n2), :]
        bi = b_ref[pl.ds(n2 + t2, n1, stride=2 * n2), :]
        y = _dot(f3_ref[...], jnp.concatenate([br, bi], axis=0).astype(BF16))
        rows = pl.ds(t2, h1, stride=n2)
        o_ref[0, rows, :] = x0_ref[0, rows, :] * (y + xv_ref[0, rows, :] * bias_ref[...])
        return carry
    lax.fori_loop(0, n2, per_t2, 0)


def _longconv(x0, xv, kf, bias, f1, m2f, m2i, f3):
    b, s, hw = x0.shape
    n2 = DFT_N2
    n1 = 2 * s // n2
    cb = LANES
    tok = pl.BlockSpec((1, s, cb), lambda j, bb: (bb, 0, j))
    return pl.pallas_call(
        functools.partial(_longconv_kernel, n1=n1),
        grid=(hw // cb, b),
        in_specs=[tok, tok,
                  pl.BlockSpec((n1, 2 * n2, cb), lambda j, bb: (0, 0, j), pipeline_mode=pl.Buffered(1)),
                  pl.BlockSpec((1, cb), lambda j, bb: (0, j)),
                  _const_spec(f1.shape), _const_spec(m2f.shape), _const_spec(m2i.shape),
                  _const_spec(f3.shape)],
        out_specs=tok,
        out_shape=jax.ShapeDtypeStruct((b, s, hw), F32),
        scratch_shapes=[pltpu.VMEM((n2 * 2 * n1, cb), F32), pltpu.VMEM((n1 * 2 * n2, cb), F32)],
        compiler_params=_cparams(2),
        name="longconv",
    )(x0, xv, kf, bias.reshape(1, hw), f1, m2f, m2i, f3)


def _bias_kernel(rpb_ref, o_ref, *, n_dc):
    var = pl.program_id(0)
    h = pl.program_id(1)
    n_keys = NA_WIN_ROWS * GRID_W
    qcol = lax.broadcasted_iota(jnp.int32, (GRID_W, n_keys), 0)
    lane = lax.broadcasted_iota(jnp.int32, (GRID_W, n_keys), 1)
    kcol = lane % GRID_W
    win = jnp.clip(qcol - NA_WIN_COLS // 2, 0, GRID_W - NA_WIN_COLS)
    ok = (kcol >= win) & (kcol < win + NA_WIN_COLS)
    dc = kcol - qcol + NA_WIN_COLS - 1
    krow = lax.broadcasted_iota(jnp.int32, (1, n_keys), 1) // GRID_W
    n_dr = 2 * NA_WIN_ROWS - 1
    acc = jnp.full((GRID_W, n_keys), NEG_BIG, F32)
    for d in range(n_dc):
        vec = jnp.zeros((1, n_keys), F32)
        for j in range(NA_WIN_ROWS):
            val = rpb_ref[(h * n_dr + (j - var + NA_WIN_ROWS - 1)) * n_dc + d]
            vec = jnp.where(krow == j, val, vec)
        acc = jnp.where(ok & (dc == d), vec, acc)
    o_ref[0, 0] = acc


def _bias_table(rpb):
    heads, n_dr, n_dc = rpb.shape
    n_keys = NA_WIN_ROWS * GRID_W
    return pl.pallas_call(
        functools.partial(_bias_kernel, n_dc=n_dc),
        grid=(NA_WIN_ROWS, heads),
        in_specs=[pl.BlockSpec(memory_space=pltpu.SMEM)],
        out_specs=pl.BlockSpec((1, 1, GRID_W, n_keys), lambda v, h: (v, h, 0, 0)),
        out_shape=jax.ShapeDtypeStruct((NA_WIN_ROWS, heads, GRID_W, n_keys), F32),
        compiler_params=_cparams(2),
        name="na_bias",
    )(rpb.reshape(-1))


def _nattn_kernel(q_ref, k_ref, v_ref, kc_ref, vc_ref, bias_ref, o_ref, *, rows, heads_per_step):
    dh = NA_HEAD_DIM
    band = NA_WIN_ROWS * GRID_W
    nt = (((1,), (1,)), ((), ()))

    def per_row(r, carry):
        rs = jnp.clip(r - NA_WIN_ROWS // 2, 0, rows - NA_WIN_ROWS)
        var = r - rs
        q = q_ref[0, pl.ds(pl.multiple_of(r * GRID_W, GRID_W), GRID_W), :]
        kb = k_ref[0, pl.ds(pl.multiple_of(rs * GRID_W, GRID_W), band), :]
        vb = v_ref[0, pl.ds(pl.multiple_of(rs * GRID_W, GRID_W), band), :]
        outs = []
        for hh in range(heads_per_step):
            sl = slice(hh * dh, (hh + 1) * dh)
            qh = q[:, sl]
            s_loc = lax.dot_general(qh, kb[:, sl], nt, preferred_element_type=F32) + bias_ref[var, hh]
            s_ctx = lax.dot_general(qh, kc_ref[0][:, sl], nt, preferred_element_type=F32)
            m = jnp.maximum(jnp.max(s_loc, axis=-1, keepdims=True), jnp.max(s_ctx, axis=-1, keepdims=True))
            p_loc = jnp.exp(s_loc - m)
            p_ctx = jnp.exp(s_ctx - m)
            den = jnp.sum(p_loc, axis=-1, keepdims=True) + jnp.sum(p_ctx, axis=-1, keepdims=True)
            o = _dot(p_loc.astype(BF16), vb[:, sl]) + _dot(p_ctx.astype(BF16), vc_ref[0][:, sl])
            outs.append(o / den)
        o_ref[0, pl.ds(pl.multiple_of(r * GRID_W, GRID_W), GRID_W), :] = (
            jnp.concatenate(outs, axis=-1).astype(BF16))
        return carry
    lax.fori_loop(0, rows, per_row, 0)


def _nattn(q, k, v, kc, vc, bias):
    b, s, aw = q.shape
    nctx = kc.shape[1]
    hps = LANES // NA_HEAD_DIM
    tok = pl.BlockSpec((1, s, LANES), lambda j, bb: (bb, 0, j))
    ctx = pl.BlockSpec((1, nctx, LANES), lambda j, bb: (bb, 0, j))
    return pl.pallas_call(
        functools.partial(_nattn_kernel, rows=s // GRID_W, heads_per_step=hps),
        grid=(aw // LANES, b),
        in_specs=[tok, tok, tok, ctx, ctx,
                  pl.BlockSpec((NA_WIN_ROWS, hps, GRID_W, NA_WIN_ROWS * GRID_W), lambda j, bb: (0, j, 0, 0))],
        out_specs=tok,
        out_shape=jax.ShapeDtypeStruct((b, s, aw), BF16),
        compiler_params=_cparams(2),
        name="nattn",
    )(q, k, v, kc, vc, bias)


def _outproj_kernel(yh_ref, ya_ref, h_ref, mod_ref, g_ref, w_ref, o_ref, *, d, hw):
    gate = mod_ref[0, :, 2 * d:3 * d]
    y = _dot(yh_ref[0].astype(BF16), w_ref[0:hw, :]) + _dot(ya_ref[0], w_ref[hw:, :])
    o_ref[0] = h_ref[0] + gate * _rms(y, g_ref[...])


def _outproj(y_hy, y_na, h, mod, g, w_out, tile):
    b, s, d = h.shape
    hw = y_hy.shape[2]
    aw = y_na.shape[2]
    tok = lambda width: pl.BlockSpec((1, tile, width), lambda bb, i: (bb, i, 0))
    return pl.pallas_call(
        functools.partial(_outproj_kernel, d=d, hw=hw),
        grid=(b, s // tile),
        in_specs=[tok(hw), tok(aw), tok(d),
                  pl.BlockSpec((1, 1, mod.shape[2]), lambda bb, i: (bb, 0, 0)),
                  _const_spec((1, d)), _const_spec(w_out.shape)],
        out_specs=tok(d),
        out_shape=jax.ShapeDtypeStruct((b, s, d), F32),
        compiler_params=_cparams(2),
        name="outproj",
    )(y_hy, y_na, h, mod, g.reshape(1, d), w_out)


def _ffn_kernel(prev_ref, main_ref, next_ref, mod_ref, gpre_ref, gpost_ref, wup_ref, cw_ref, cb_ref, wdn_ref,
                o_ref, hn_ref, z_ref, acc_ref, *, tile, d, n_chunks):
    fc = FFN_CHUNK
    shift = mod_ref[0, :, 3 * d:4 * d]
    scale = mod_ref[0, :, 4 * d:5 * d]
    gate = mod_ref[0, :, 5 * d:6 * d]
    g = gpre_ref[...]
    _fill_halo_tile(hn_ref, prev_ref, main_ref, next_ref,
                    lambda t: _mod_norm(t, g, scale, shift), tile)
    acc_ref[...] = jnp.zeros_like(acc_ref)

    def chunk(j, carry):
        z_ref[...] = _dot(hn_ref[...], wup_ref[j])
        cw = cw_ref[j]
        c = (cw[0:1, :] * z_ref[pl.ds(HALO - 1, tile), :]
             + cw[1:2, :] * z_ref[pl.ds(HALO, tile), :]
             + cw[2:3, :] * z_ref[pl.ds(HALO + 1, tile), :]
             + cb_ref[j])
        a = (_silu(c[:, 0:fc]) * c[:, fc:2 * fc]).astype(BF16)
        acc_ref[...] += _dot(a, wdn_ref[j])
        return carry
    lax.fori_loop(0, n_chunks, chunk, 0)
    o_ref[0] = main_ref[0] + gate * _rms(acc_ref[...], gpost_ref[...])


def _ffn(h, mod, g_pre, g_post, w_up, w_dw, b_dw, w_down, tile):
    b, s, d = h.shape
    dff = w_down.shape[0]
    fc = FFN_CHUNK
    nch = dff // fc
    pair = lambda t: jnp.concatenate([t[..., :dff].reshape(t.shape[:-1] + (nch, fc)),
                                      t[..., dff:].reshape(t.shape[:-1] + (nch, fc))], axis=-1)
    wup = jnp.transpose(pair(w_up), (1, 0, 2)).astype(BF16)
    cw = jnp.transpose(pair(w_dw), (1, 0, 2))
    cb = pair(b_dw).reshape(nch, 1, 2 * fc)
    wdn = w_down.reshape(nch, fc, d).astype(BF16)
    return pl.pallas_call(
        functools.partial(_ffn_kernel, tile=tile, d=d, n_chunks=nch),
        grid=(b, s // tile),
        in_specs=_halo_specs(tile, d, s) + [
            pl.BlockSpec((1, 1, mod.shape[2]), lambda bb, i: (bb, 0, 0)),
            _const_spec((1, d)), _const_spec((1, d)),
            _const_spec(wup.shape), _const_spec(cw.shape), _const_spec(cb.shape), _const_spec(wdn.shape)],
        out_specs=pl.BlockSpec((1, tile, d), lambda bb, i: (bb, i, 0)),
        out_shape=jax.ShapeDtypeStruct((b, s, d), F32),
        scratch_shapes=[pltpu.VMEM((tile + 2 * HALO, d), BF16),
                        pltpu.VMEM((tile + 2 * HALO, 2 * fc), F32),
                        pltpu.VMEM((tile, d), F32)],
        compiler_params=_cparams(2),
        name="convffn",
    )(h, h, h, mod, g_pre.reshape(1, d), g_post.reshape(1, d), wup, cw, cb, wdn)


CONF_ROWS = 32
CONF_COLS = 256


def _conformer_kernel(prev_ref, main_ref, next_ref, mod_ref, gpre_ref, gpost_ref, w1_ref, b1_ref,
                      dw_ref, db_ref, lg_ref, lb_ref, w2_ref, b2_ref, o_ref, hn_ref, u_ref, c_ref,
                      *, tile, d, seq):
    shift = mod_ref[0, :, 0:d]
    scale = mod_ref[0, :, d:2 * d]
    gate = mod_ref[0, :, 2 * d:3 * d]
    g = gpre_ref[...]
    _fill_halo_tile(hn_ref, prev_ref, main_ref, next_ref,
                    lambda t: _mod_norm(t, g, scale, shift), tile)
    ag = _dot(hn_ref[...], w1_ref[...]) + b1_ref[...]
    pos = (pl.program_id(1) * tile - HALO
           + lax.broadcasted_iota(jnp.int32, (tile + 2 * HALO, 1), 0))
    u_ref[...] = jnp.where((pos >= 0) & (pos < seq), ag[:, 0:d] * jax.nn.sigmoid(ag[:, d:2 * d]), 0.0)
    first = HALO - CONF_K // 2
    for r0 in range(0, tile, CONF_ROWS):
        for c0 in range(0, d, CONF_COLS):
            acc = jnp.zeros((CONF_ROWS, CONF_COLS), F32) + db_ref[:, c0:c0 + CONF_COLS]
            for k in range(CONF_K):
                acc = acc + dw_ref[k:k + 1, c0:c0 + CONF_COLS] * u_ref[pl.ds(r0 + first + k, CONF_ROWS),
                                                                      c0:c0 + CONF_COLS]
            c_ref[r0:r0 + CONF_ROWS, c0:c0 + CONF_COLS] = acc
    c = c_ref[...]
    mu = jnp.mean(c, axis=-1, keepdims=True)
    cc = c - mu
    var = jnp.mean(cc * cc, axis=-1, keepdims=True)
    ln = cc * lax.rsqrt(var + EPS) * lg_ref[...] + lb_ref[...]
    y = _dot(_silu(ln).astype(BF16), w2_ref[...]) + b2_ref[...]
    o_ref[0] = main_ref[0] + gate * _rms(y, gpost_ref[...])


def _conformer(h, mod, g_pre, g_post, w_pw1, b_pw1, w_dw, b_dw, ln_g, ln_b, w_pw2, b_pw2, tile):
    b, s, d = h.shape
    row = lambda t: t.reshape(1, -1)
    return pl.pallas_call(
        functools.partial(_conformer_kernel, tile=tile, d=d, seq=s),
        grid=(b, s // tile),
        in_specs=_halo_specs(tile, d, s) + [
            pl.BlockSpec((1, 1, mod.shape[2]), lambda bb, i: (bb, 0, 0)),
            _const_spec((1, d)), _const_spec((1, d)),
            _const_spec(w_pw1.shape), _const_spec((1, 2 * d)),
            _const_spec(w_dw.shape), _const_spec((1, d)), _const_spec((1, d)), _const_spec((1, d)),
            _const_spec(w_pw2.shape), _const_spec((1, d))],
        out_specs=pl.BlockSpec((1, tile, d), lambda bb, i: (bb, i, 0)),
        out_shape=jax.ShapeDtypeStruct((b, s, d), F32),
        scratch_shapes=[pltpu.VMEM((tile + 2 * HALO, d), BF16),
                        pltpu.VMEM((tile + 2 * HALO, d), F32),
                        pltpu.VMEM((tile, d), F32)],
        compiler_params=_cparams(2),
        name="conformer",
    )(h, h, h, mod, row(g_pre), row(g_post), w_pw1.astype(BF16), row(b_pw1), w_dw, row(b_dw),
      row(ln_g), row(ln_b), w_pw2.astype(BF16), row(b_pw2))


def _token_tile(seq):
    return min(seq, 512)


def kernel(x, c, ctx, c_ctx, w_mod, b_mod, g_mix_pre, g_mix_post, g_ffn_pre, g_ffn_post, w_in, w_out, hy_short_w, hy_short_b, hy_f_w1, hy_f_b1, hy_f_w2, hy_f_b2, hy_f_w3, hy_f_b3, hy_f_w4, hy_f_freq, hy_bias, na_rpb, cf_w_pw1, cf_b_pw1, cf_w_dw, cf_b_dw, cf_ln_g, cf_ln_b, cf_w_pw2, cf_b_pw2, ffn_w_up, ffn_w_dw, ffn_b_dw, ffn_w_down):
    bsz, seq, d = x.shape
    depth = w_mod.shape[0]
    hw = hy_bias.shape[1]
    aw = NA_HEADS * NA_HEAD_DIM
    tile = _token_tile(seq)
    assert seq % tile == 0 and seq % (GRID_W * NA_WIN_ROWS) == 0 and tile % HALO == 0

    n_rows = -(-(bsz + 1) // BF16_SUBLANES) * BF16_SUBLANES
    cc = jnp.concatenate([c, c_ctx[None, :], jnp.zeros((n_rows - bsz - 1, d), F32)], axis=0)
    mod_all = _adaln(cc, w_mod, b_mod)

    h = x
    for layer in range(depth):
        mod = mod_all[layer, :bsz].reshape(bsz, 1, 6 * d)
        if layer % 2 == 0:
            e = layer // 2
            w_e = w_in[e].astype(BF16)
            k_ctx, v_ctx = _ctxkv(ctx, mod_all[layer, bsz:bsz + 1, :2 * d], g_mix_pre[layer],
                                  w_e[:, 3 * hw + aw:])
            x0, xv, q, k, v = _inproj(h, mod, g_mix_pre[layer], w_e[:, :3 * hw], w_e[:, 3 * hw:],
                                      hy_short_w[e], hy_short_b[e], tile)
            f1, m2f, m2i, f3 = _dft_tables(seq)
            hf, hb = _hyena_filters(seq, hy_f_w1[e], hy_f_b1[e], hy_f_w2[e], hy_f_b2[e],
                                    hy_f_w3[e], hy_f_b3[e], hy_f_w4[e], hy_f_freq[e])
            kf = _filter_spectrum(hf, hb, f1, m2f)
            y_hy = _longconv(x0, xv, kf, hy_bias[e], f1, m2f, m2i, f3)
            y_na = _nattn(q, k, v, k_ctx, v_ctx, _bias_table(na_rpb[e]))
            h = _outproj(y_hy, y_na, h, mod, g_mix_post[layer], w_out[e].astype(BF16), tile)
        else:
            o = layer // 2
            h = _conformer(h, mod, g_mix_pre[layer], g_mix_post[layer], cf_w_pw1[o], cf_b_pw1[o],
                           cf_w_dw[o], cf_b_dw[o], cf_ln_g[o], cf_ln_b[o], cf_w_pw2[o], cf_b_pw2[o], tile)
        h = _ffn(h, mod, g_ffn_pre[layer], g_ffn_post[layer], ffn_w_up[layer], ffn_w_dw[layer],
                 ffn_b_dw[layer], ffn_w_down[layer], tile)
    return h
```

```python
import functools
import math

import numpy as np
import jax
import jax.numpy as jnp
from jax import lax
from jax.experimental import pallas as pl
from jax.experimental.pallas import tpu as pltpu

F32 = jnp.float32
BF16 = jnp.bfloat16

EPS = 1e-6
NEG_BIG = -1e30

GRID_W = 64
NA_HEADS = 8
NA_HEAD_DIM = 64
NA_WIN_ROWS = 8
NA_WIN_COLS = 16
HY_SHORT_K = 3
HY_EMB_DIM = 33
HY_DECAY_FAST = 0.3
HY_DECAY_SLOW = 1.5
HY_DECAY_TARGET = 1e-2
CONF_K = 31
FFN_CONV_K = 3

LANES = 128
F32_SUBLANES = 8
BF16_SUBLANES = 16
VMEM_LIMIT = 56 * 1024 * 1024

HALO = BF16_SUBLANES
DFT_N2 = 64
DFT_UNROLL = 16
NA_UNROLL = 2
FFN_CHUNK = 256
EMB_PAD = LANES


def _cparams(n_axes):
    return pltpu.CompilerParams(dimension_semantics=("arbitrary",) * n_axes,
                                vmem_limit_bytes=VMEM_LIMIT)


def _const_spec(shape):
    nd = len(shape)
    return pl.BlockSpec(shape, lambda *_: (0,) * nd, pipeline_mode=pl.Buffered(1))


def _silu(t):
    return t * jax.nn.sigmoid(t)


def _mod_norm(t, g, scale, shift):
    ms = jnp.mean(t * t, axis=-1, keepdims=True)
    return (t * lax.rsqrt(ms + EPS) * g) * (1.0 + scale) + shift


def _rms(t, g):
    ms = jnp.mean(t * t, axis=-1, keepdims=True)
    return t * lax.rsqrt(ms + EPS) * g


def _dot(a, b):
    return jnp.dot(a, b, preferred_element_type=F32)


def _dot_hi(a, b):
    return jnp.dot(a, b, preferred_element_type=F32, precision=lax.Precision.HIGHEST)


def _fill_halo_tile(hn_ref, prev_ref, main_ref, next_ref, norm_fn, tile):
    i = pl.program_id(1)
    last = pl.num_programs(1) - 1
    p = norm_fn(prev_ref[0])
    n = norm_fn(next_ref[0])
    hn_ref[0:HALO, :] = jnp.where(i > 0, p, 0.0).astype(BF16)
    hn_ref[HALO:HALO + tile, :] = norm_fn(main_ref[0]).astype(BF16)
    hn_ref[HALO + tile:, :] = jnp.where(i < last, n, 0.0).astype(BF16)


def _halo_specs(tile, d, seq):
    r = tile // HALO
    nblk = seq // HALO
    return [
        pl.BlockSpec((1, HALO, d), lambda b, i: (b, jnp.maximum(i * r - 1, 0), 0)),
        pl.BlockSpec((1, tile, d), lambda b, i: (b, i, 0)),
        pl.BlockSpec((1, HALO, d), lambda b, i: (b, jnp.minimum((i + 1) * r, nblk - 1), 0)),
    ]


def _adaln_kernel(c_ref, w_ref, b_ref, o_ref):
    s = _silu(c_ref[...]).astype(BF16)
    o_ref[0] = _dot(s, w_ref[0].astype(BF16)) + b_ref[0]


def _adaln(cc, w_mod, b_mod):
    depth, d, n6 = w_mod.shape
    r = cc.shape[0]
    nb = d
    return pl.pallas_call(
        _adaln_kernel,
        grid=(depth, n6 // nb),
        in_specs=[pl.BlockSpec((r, d), lambda l, j: (0, 0)),
                  pl.BlockSpec((1, d, nb), lambda l, j: (l, 0, j)),
                  pl.BlockSpec((1, 1, nb), lambda l, j: (l, 0, j))],
        out_specs=pl.BlockSpec((1, r, nb), lambda l, j: (l, 0, j)),
        out_shape=jax.ShapeDtypeStruct((depth, r, n6), F32),
        compiler_params=_cparams(2),
        name="adaln",
    )(cc, w_mod, b_mod.reshape(depth, 1, n6))


def _inproj_kernel(prev_ref, main_ref, next_ref, mod_ref, g_ref, why_ref, wqkv_ref, sw_ref, sb_ref,
                   x0_ref, xv_ref, q_ref, k_ref, v_ref, hn_ref, z_ref, *, tile, d, hw, aw):
    shift = mod_ref[0, :, 0:d]
    scale = mod_ref[0, :, d:2 * d]
    g = g_ref[...]
    _fill_halo_tile(hn_ref, prev_ref, main_ref, next_ref,
                    lambda t: _mod_norm(t, g, scale, shift), tile)
    z_ref[...] = _dot(hn_ref[...], why_ref[...])
    uc = (sw_ref[0:1, :] * z_ref[pl.ds(HALO - 1, tile), :]
          + sw_ref[1:2, :] * z_ref[pl.ds(HALO, tile), :]
          + sw_ref[2:3, :] * z_ref[pl.ds(HALO + 1, tile), :]
          + sb_ref[...])
    x0_ref[0] = uc[:, 0:hw]
    xv_ref[0] = uc[:, hw:2 * hw] * uc[:, 2 * hw:3 * hw]
    qkv = _dot(hn_ref[pl.ds(HALO, tile), :], wqkv_ref[...])
    q_ref[0] = (qkv[:, 0:aw] * (NA_HEAD_DIM ** -0.5)).astype(BF16)
    k_ref[0] = qkv[:, aw:2 * aw].astype(BF16)
    v_ref[0] = qkv[:, 2 * aw:3 * aw].astype(BF16)


def _inproj(h, mod, g, w_hy, w_qkv, short_w, short_b, tile):
    b, s, d = h.shape
    hw = w_hy.shape[1] // 3
    aw = w_qkv.shape[1] // 3
    tok = lambda width: pl.BlockSpec((1, tile, width), lambda bb, i: (bb, i, 0))
    return pl.pallas_call(
        functools.partial(_inproj_kernel, tile=tile, d=d, hw=hw, aw=aw),
        grid=(b, s // tile),
        in_specs=_halo_specs(tile, d, s) + [
            pl.BlockSpec((1, 1, mod.shape[2]), lambda bb, i: (bb, 0, 0)),
            _const_spec((1, d)),
            _const_spec(w_hy.shape),
            _const_spec(w_qkv.shape),
            _const_spec(short_w.shape),
            _const_spec((1, 3 * hw)),
        ],
        out_specs=[tok(hw), tok(hw), tok(aw), tok(aw), tok(aw)],
        out_shape=[jax.ShapeDtypeStruct((b, s, hw), F32), jax.ShapeDtypeStruct((b, s, hw), F32),
                   jax.ShapeDtypeStruct((b, s, aw), BF16), jax.ShapeDtypeStruct((b, s, aw), BF16),
                   jax.ShapeDtypeStruct((b, s, aw), BF16)],
        scratch_shapes=[pltpu.VMEM((tile + 2 * HALO, d), BF16),
                        pltpu.VMEM((tile + 2 * HALO, 3 * hw), F32)],
        compiler_params=_cparams(2),
        name="inproj",
    )(h, h, h, mod, g.reshape(1, d), w_hy, w_qkv, short_w, short_b.reshape(1, 3 * hw))


def _ctxkv_kernel(ctx_ref, mod_ref, g_ref, w_ref, k_ref, v_ref, *, d, aw):
    shift = mod_ref[:, 0:d]
    scale = mod_ref[:, d:2 * d]
    cn = _mod_norm(ctx_ref[0], g_ref[...], scale, shift).astype(BF16)
    kv = _dot(cn, w_ref[...])
    k_ref[0] = kv[:, 0:aw].astype(BF16)
    v_ref[0] = kv[:, aw:2 * aw].astype(BF16)


def _ctxkv(ctx, mod_ctx, g, w_kv):
    b, n, d = ctx.shape
    aw = w_kv.shape[1] // 2
    out = pl.BlockSpec((1, n, aw), lambda bb: (bb, 0, 0))
    return pl.pallas_call(
        functools.partial(_ctxkv_kernel, d=d, aw=aw),
        grid=(b,),
        in_specs=[pl.BlockSpec((1, n, d), lambda bb: (bb, 0, 0)),
                  _const_spec(mod_ctx.shape), _const_spec((1, d)), _const_spec(w_kv.shape)],
        out_specs=[out, out],
        out_shape=[jax.ShapeDtypeStruct((b, n, aw), BF16)] * 2,
        compiler_params=_cparams(1),
        name="ctxkv",
    )(ctx, mod_ctx, g.reshape(1, d), w_kv)


def _filter_kernel(z_ref, w1_ref, b1_ref, w2_ref, b2_ref, w3_ref, b3_ref, w4_ref, fr_ref, dl_ref,
                   hf_ref, hb_ref, *, rows, hw):
    z = z_ref[...]
    hdn = jnp.sin(fr_ref[0:1, :] * (_dot_hi(z, w1_ref[...]) + b1_ref[...]))
    hdn = jnp.sin(fr_ref[1:2, :] * (_dot_hi(hdn, w2_ref[...]) + b2_ref[...]))
    hdn = jnp.sin(fr_ref[2:3, :] * (_dot_hi(hdn, w3_ref[...]) + b3_ref[...]))
    filt = _dot_hi(hdn, w4_ref[...])
    decay = jnp.exp(-z[:, 0:1] * dl_ref[...])
    hf_ref[...] = filt[:, 0:hw] * decay
    pos = pl.program_id(0) * rows + lax.broadcasted_iota(jnp.int32, (rows, 1), 0)
    hb_ref[...] = jnp.where(pos > 0, filt[:, hw:2 * hw] * decay, 0.0)


def _hyena_filters(length, w1, b1, w2, b2, w3, b3, w4, freq):
    hid = w1.shape[1]
    hw = w4.shape[1] // 2
    bands = (HY_EMB_DIM - 1) // 2
    t01 = jnp.linspace(0.0, 1.0, length, dtype=F32)[:, None]
    w_pos = 2.0 * math.pi * jnp.arange(length, dtype=F32) / length
    f = jnp.linspace(1e-4, bands - 1, bands, dtype=F32)
    ang = w_pos[:, None] * f[None, :]
    z = jnp.concatenate([t01, jnp.cos(ang), -jnp.sin(ang),
                         jnp.zeros((length, EMB_PAD - HY_EMB_DIM), F32)], axis=-1)
    w1p = jnp.concatenate([w1, jnp.zeros((EMB_PAD - HY_EMB_DIM, hid), F32)], axis=0)
    deltas = jnp.abs(jnp.linspace(math.log(HY_DECAY_TARGET) / HY_DECAY_FAST,
                                  math.log(HY_DECAY_TARGET) / HY_DECAY_SLOW, hw, dtype=F32))[None, :]
    rows = min(length, 1024)
    out = pl.BlockSpec((rows, hw), lambda i: (i, 0))
    return pl.pallas_call(
        functools.partial(_filter_kernel, rows=rows, hw=hw),
        grid=(length // rows,),
        in_specs=[pl.BlockSpec((rows, EMB_PAD), lambda i: (i, 0)),
                  _const_spec(w1p.shape), _const_spec((1, hid)),
                  _const_spec(w2.shape), _const_spec((1, hid)),
                  _const_spec(w3.shape), _const_spec((1, hid)),
                  _const_spec(w4.shape), _const_spec(freq.shape), _const_spec(deltas.shape)],
        out_specs=[out, out],
        out_shape=[jax.ShapeDtypeStruct((length, hw), F32)] * 2,
        compiler_params=_cparams(1),
        name="hyena_filter",
    )(z, w1p, b1.reshape(1, hid), w2, b2.reshape(1, hid), w3, b3.reshape(1, hid), w4, freq, deltas)


def _dft_tables(seq):
    n2 = DFT_N2
    h1 = seq // n2
    n1 = 2 * h1
    n = n1 * n2
    k1 = np.arange(n1)[:, None]
    t1 = np.arange(h1)[None, :]
    a1 = 2.0 * np.pi * ((k1 * t1) % n1) / n1
    f1 = np.concatenate([np.cos(a1), -np.sin(a1)], axis=0)
    f3 = np.concatenate([np.cos(a1).T, -np.sin(a1).T], axis=1) / n
    kk1 = np.arange(n1)[:, None, None]
    k2 = np.arange(n2)[None, :, None]
    t2 = np.arange(n2)[None, None, :]
    th = 2.0 * np.pi * ((t2 * k2 * n1 + t2 * kk1) % n) / n
    wr, wi = np.cos(th), -np.sin(th)
    m2f = np.concatenate([np.concatenate([wr, -wi], axis=2),
                          np.concatenate([wi, wr], axis=2)], axis=1)
    m2i = np.transpose(m2f, (0, 2, 1))
    as_bf16 = lambda a: jnp.asarray(a, F32).astype(BF16)
    return as_bf16(f1), as_bf16(m2f), as_bf16(m2i), as_bf16(f3)


def _pitch(rows):
    return rows + F32_SUBLANES


def _dft_stage1(load_rows, f1_ref, a_ref, n1):
    def body(t2, carry):
        p = _dot(f1_ref[...], load_rows(t2).astype(BF16))
        a_ref[pl.ds(pl.multiple_of(t2 * _pitch(2 * n1), F32_SUBLANES), 2 * n1), :] = p
        return carry
    lax.fori_loop(0, DFT_N2, body, 0, unroll=DFT_UNROLL)


def _dft_stage2(a_ref, m2f_ref, k1, n1):
    ar = a_ref[pl.ds(k1, DFT_N2, stride=_pitch(2 * n1)), :]
    ai = a_ref[pl.ds(n1 + k1, DFT_N2, stride=_pitch(2 * n1)), :]
    return _dot(m2f_ref[k1], jnp.concatenate([ar, ai], axis=0).astype(BF16))


def _spectrum_kernel(hf_ref, hb_ref, f1_ref, m2f_ref, kf_ref, af_ref, ab_ref, *, n1):
    n2 = DFT_N2
    _dft_stage1(lambda t2: hf_ref[pl.ds(t2, n1 // 2, stride=n2), :], f1_ref, af_ref, n1)
    _dft_stage1(lambda t2: hb_ref[pl.ds(t2, n1 // 2, stride=n2), :], f1_ref, ab_ref, n1)

    def body(k1, carry):
        xf = _dft_stage2(af_ref, m2f_ref, k1, n1)
        xb = _dft_stage2(ab_ref, m2f_ref, k1, n1)
        sign = jnp.where(lax.broadcasted_iota(jnp.int32, (2 * n2, 1), 0) < n2, 1.0, -1.0)
        kf_ref[k1] = xf + sign * xb
        return carry
    lax.fori_loop(0, n1, body, 0, unroll=DFT_UNROLL)


def _filter_spectrum(hf, hb, f1, m2f):
    s, hw = hf.shape
    n2 = DFT_N2
    n1 = 2 * s // n2
    cb = LANES
    col = pl.BlockSpec((s, cb), lambda j: (0, j))
    return pl.pallas_call(
        functools.partial(_spectrum_kernel, n1=n1),
        grid=(hw // cb,),
        in_specs=[col, col, _const_spec(f1.shape), _const_spec(m2f.shape)],
        out_specs=pl.BlockSpec((n1, 2 * n2, cb), lambda j: (0, 0, j)),
        out_shape=jax.ShapeDtypeStruct((n1, 2 * n2, hw), F32),
        scratch_shapes=[pltpu.VMEM((n2 * _pitch(2 * n1), cb), F32), pltpu.VMEM((n2 * _pitch(2 * n1), cb), F32)],
        compiler_params=_cparams(1),
        name="filter_spectrum",
    )(hf, hb, f1, m2f)


def _longconv_kernel(x0_ref, xv_ref, kf_ref, bias_ref, f1_ref, m2f_ref, m2i_ref, f3_ref, o_ref,
                     a_ref, b_ref, *, n1):
    n2 = DFT_N2
    h1 = n1 // 2
    _dft_stage1(lambda t2: xv_ref[0, pl.ds(t2, h1, stride=n2), :], f1_ref, a_ref, n1)

    def per_k1(k1, carry):
        x = _dft_stage2(a_ref, m2f_ref, k1, n1)
        kf = kf_ref[k1]
        xr, xi = x[0:n2], x[n2:2 * n2]
        kr, ki = kf[0:n2], kf[n2:2 * n2]
        y = jnp.concatenate([xr * kr - xi * ki, xr * ki + xi * kr], axis=0).astype(BF16)
        b_ref[pl.ds(pl.multiple_of(k1 * _pitch(2 * n2), F32_SUBLANES), 2 * n2), :] = _dot(m2i_ref[k1], y)
        return carry
    lax.fori_loop(0, n1, per_k1, 0, unroll=DFT_UNROLL)

    def per_t2(t2, carry):
        br = b_ref[pl.ds(t2, n1, stride=_pitch(2 * n2)), :]
        bi = b_ref[pl.ds(n2 + t2, n1, stride=_pitch(2 * n2)), :]
        y = _dot(f3_ref[...], jnp.concatenate([br, bi], axis=0).astype(BF16))
        rows = pl.ds(t2, h1, stride=n2)
        o_ref[0, rows, :] = x0_ref[0, rows, :] * (y + xv_ref[0, rows, :] * bias_ref[...])
        return carry
    lax.fori_loop(0, n2, per_t2, 0, unroll=DFT_UNROLL)


def _longconv(x0, xv, kf, bias, f1, m2f, m2i, f3):
    b, s, hw = x0.shape
    n2 = DFT_N2
    n1 = 2 * s // n2
    cb = LANES
    tok = pl.BlockSpec((1, s, cb), lambda j, bb: (bb, 0, j))
    return pl.pallas_call(
        functools.partial(_longconv_kernel, n1=n1),
        grid=(hw // cb, b),
        in_specs=[tok, tok,
                  pl.BlockSpec((n1, 2 * n2, cb), lambda j, bb: (0, 0, j), pipeline_mode=pl.Buffered(1)),
                  pl.BlockSpec((1, cb), lambda j, bb: (0, j)),
                  _const_spec(f1.shape), _const_spec(m2f.shape), _const_spec(m2i.shape),
                  _const_spec(f3.shape)],
        out_specs=tok,
        out_shape=jax.ShapeDtypeStruct((b, s, hw), F32),
        scratch_shapes=[pltpu.VMEM((n2 * _pitch(2 * n1), cb), F32), pltpu.VMEM((n1 * _pitch(2 * n2), cb), F32)],
        compiler_params=_cparams(2),
        name="longconv",
    )(x0, xv, kf, bias.reshape(1, hw), f1, m2f, m2i, f3)


NA_QROWS = NA_WIN_ROWS // 2
NA_BAND_ROWS = NA_QROWS + NA_WIN_ROWS


def _bias_kernel(rpb_ref, o_ref, *, n_dc):
    h = pl.program_id(0)
    n_dr = 2 * NA_WIN_ROWS - 1
    qcol = lax.broadcasted_iota(jnp.int32, (GRID_W, LANES), 0)
    lane = lax.broadcasted_iota(jnp.int32, (GRID_W, LANES), 1)
    kcol = lane % GRID_W
    win = jnp.clip(qcol - NA_WIN_COLS // 2, 0, GRID_W - NA_WIN_COLS)
    col_ok = (kcol >= win) & (kcol < win + NA_WIN_COLS)
    low_half = lane < GRID_W
    lane_row = lax.broadcasted_iota(jnp.int32, (1, LANES), 1)
    toeplitz = []
    for dr in range(n_dr):
        vec = jnp.zeros((1, LANES), F32)
        for d in range(n_dc):
            vec = jnp.where(lane_row == d, rpb_ref[(h * n_dr + dr) * n_dc + d], vec)
        rows = jnp.broadcast_to(vec, (GRID_W, LANES))
        toeplitz.append([pltpu.roll(rows, (LANES - (NA_WIN_COLS - 1) + half * GRID_W) % LANES, 1,
                                    stride=1, stride_axis=0) for half in range(2)])
    neg = jnp.full((GRID_W, LANES), NEG_BIG, F32)
    for pos in range(3):
        for p in range(NA_QROWS):
            lo = (0, p, NA_QROWS)[pos]
            for jj in range(NA_BAND_ROWS // 2):
                halves = []
                for half in range(2):
                    j = 2 * jj + half
                    dr = j - p + NA_WIN_ROWS - 1 - NA_QROWS * pos
                    halves.append(toeplitz[dr][half] if lo <= j < lo + NA_WIN_ROWS else neg)
                tile = jnp.where(low_half, halves[0], halves[1])
                o_ref[pos, 0, p * GRID_W:(p + 1) * GRID_W, jj * LANES:(jj + 1) * LANES] = (
                    jnp.where(col_ok, tile, NEG_BIG))


def _bias_table(rpb):
    heads, n_dr, n_dc = rpb.shape
    blk = (NA_QROWS * GRID_W, NA_BAND_ROWS * GRID_W)
    assert 2 * GRID_W == LANES and NA_BAND_ROWS % 2 == 0 and n_dc <= LANES
    return pl.pallas_call(
        functools.partial(_bias_kernel, n_dc=n_dc),
        grid=(heads,),
        in_specs=[pl.BlockSpec(memory_space=pltpu.SMEM)],
        out_specs=pl.BlockSpec((3, 1) + blk, lambda h: (0, h, 0, 0)),
        out_shape=jax.ShapeDtypeStruct((3, heads) + blk, F32),
        compiler_params=_cparams(1),
        name="na_bias",
    )(rpb.reshape(-1))


def _nattn_kernel(q_ref, k_ref, v_ref, kc_ref, vc_ref, bias_ref, o_ref, vs_ref, vcs_ref, *, rows):
    dh = NA_HEAD_DIM
    nq = NA_QROWS * GRID_W
    band = NA_BAND_ROWS * GRID_W
    nt = (((1,), (1,)), ((), ()))

    def own_lanes(n_rows):
        lane = lax.broadcasted_iota(jnp.int32, (n_rows, LANES), 1)
        return [lane < dh, lane >= dh]

    for hh in range(2):
        vs_ref[hh] = jnp.where(own_lanes(v_ref.shape[1])[hh], v_ref[0], 1.0).astype(BF16)
        vcs_ref[hh] = jnp.where(own_lanes(vc_ref.shape[1])[hh], vc_ref[0], 1.0).astype(BF16)
    own_q = own_lanes(nq)
    n_blocks = rows // NA_QROWS

    def per_block(i, carry):
        r0 = i * NA_QROWS
        pos = jnp.where(i == 0, 0, jnp.where(i == n_blocks - 1, 2, 1))
        key0 = pl.multiple_of(jnp.clip(r0 - NA_WIN_ROWS // 2, 0, rows - NA_BAND_ROWS) * GRID_W, GRID_W)
        qrows = pl.ds(pl.multiple_of(r0 * GRID_W, nq), nq)
        q = q_ref[0, qrows, :]
        kb = k_ref[0, pl.ds(key0, band), :]
        outs = []
        for hh in range(2):
            qh = jnp.where(own_q[hh], q, 0.0).astype(BF16)
            s_loc = lax.dot_general(qh, kb, nt, preferred_element_type=F32) + bias_ref[pos, hh]
            s_ctx = lax.dot_general(qh, kc_ref[0], nt, preferred_element_type=F32)
            s = jnp.concatenate([s_loc, s_ctx], axis=-1)
            p = jnp.exp(s - jnp.max(s, axis=-1, keepdims=True)).astype(BF16)
            o = _dot(p[:, 0:band], vs_ref[hh, pl.ds(key0, band), :]) + _dot(p[:, band:], vcs_ref[hh])
            outs.append(o / pltpu.roll(o, dh, axis=1))
        o_ref[0, qrows, :] = jnp.where(own_q[0], outs[0], outs[1]).astype(BF16)
        return carry
    lax.fori_loop(0, n_blocks, per_block, 0, unroll=NA_UNROLL)


def _nattn(q, k, v, kc, vc, bias):
    b, s, aw = q.shape
    nctx = kc.shape[1]
    hps = LANES // NA_HEAD_DIM
    assert (s // GRID_W) % NA_QROWS == 0 and s // GRID_W >= NA_BAND_ROWS
    tok = pl.BlockSpec((1, s, LANES), lambda j, bb: (bb, 0, j))
    ctx = pl.BlockSpec((1, nctx, LANES), lambda j, bb: (bb, 0, j))
    return pl.pallas_call(
        functools.partial(_nattn_kernel, rows=s // GRID_W),
        grid=(aw // LANES, b),
        in_specs=[tok, tok, tok, ctx, ctx,
                  pl.BlockSpec((3, hps) + bias.shape[2:], lambda j, bb: (0, j, 0, 0))],
        out_specs=tok,
        out_shape=jax.ShapeDtypeStruct((b, s, aw), BF16),
        scratch_shapes=[pltpu.VMEM((hps, s, LANES), BF16), pltpu.VMEM((hps, nctx, LANES), BF16)],
        compiler_params=_cparams(2),
        name="nattn",
    )(q, k, v, kc, vc, bias)


def _outproj_kernel(yh_ref, ya_ref, h_ref, mod_ref, g_ref, w_ref, o_ref, *, d, hw):
    gate = mod_ref[0, :, 2 * d:3 * d]
    y = _dot(yh_ref[0].astype(BF16), w_ref[0:hw, :]) + _dot(ya_ref[0], w_ref[hw:, :])
    o_ref[0] = h_ref[0] + gate * _rms(y, g_ref[...])


def _outproj(y_hy, y_na, h, mod, g, w_out, tile):
    b, s, d = h.shape
    hw = y_hy.shape[2]
    aw = y_na.shape[2]
    tok = lambda width: pl.BlockSpec((1, tile, width), lambda bb, i: (bb, i, 0))
    return pl.pallas_call(
        functools.partial(_outproj_kernel, d=d, hw=hw),
        grid=(b, s // tile),
        in_specs=[tok(hw), tok(aw), tok(d),
                  pl.BlockSpec((1, 1, mod.shape[2]), lambda bb, i: (bb, 0, 0)),
                  _const_spec((1, d)), _const_spec(w_out.shape)],
        out_specs=tok(d),
        out_shape=jax.ShapeDtypeStruct((b, s, d), F32),
        compiler_params=_cparams(2),
        name="outproj",
    )(y_hy, y_na, h, mod, g.reshape(1, d), w_out)


def _ffn_kernel(prev_ref, main_ref, next_ref, mod_ref, gpre_ref, gpost_ref, wup_ref, cw_ref, cb_ref, wdn_ref,
                o_ref, hn_ref, z_ref, acc_ref, *, tile, d, n_chunks):
    fc = FFN_CHUNK
    shift = mod_ref[0, :, 3 * d:4 * d]
    scale = mod_ref[0, :, 4 * d:5 * d]
    gate = mod_ref[0, :, 5 * d:6 * d]
    g = gpre_ref[...]
    _fill_halo_tile(hn_ref, prev_ref, main_ref, next_ref,
                    lambda t: _mod_norm(t, g, scale, shift), tile)
    acc_ref[...] = jnp.zeros_like(acc_ref)

    def chunk(j, carry):
        z_ref[...] = _dot(hn_ref[...], wup_ref[j])
        cw = cw_ref[j]
        c = (cw[0:1, :] * z_ref[pl.ds(HALO - 1, tile), :]
             + cw[1:2, :] * z_ref[pl.ds(HALO, tile), :]
             + cw[2:3, :] * z_ref[pl.ds(HALO + 1, tile), :]
             + cb_ref[j])
        a = (_silu(c[:, 0:fc]) * c[:, fc:2 * fc]).astype(BF16)
        acc_ref[...] += _dot(a, wdn_ref[j])
        return carry
    lax.fori_loop(0, n_chunks, chunk, 0)
    o_ref[0] = main_ref[0] + gate * _rms(acc_ref[...], gpost_ref[...])


def _ffn(h, mod, g_pre, g_post, w_up, w_dw, b_dw, w_down, tile):
    b, s, d = h.shape
    dff = w_down.shape[0]
    fc = FFN_CHUNK
    nch = dff // fc
    pair = lambda t: jnp.concatenate([t[..., :dff].reshape(t.shape[:-1] + (nch, fc)),
                                      t[..., dff:].reshape(t.shape[:-1] + (nch, fc))], axis=-1)
    wup = jnp.transpose(pair(w_up), (1, 0, 2)).astype(BF16)
    cw = jnp.transpose(pair(w_dw), (1, 0, 2))
    cb = pair(b_dw).reshape(nch, 1, 2 * fc)
    wdn = w_down.reshape(nch, fc, d).astype(BF16)
    return pl.pallas_call(
        functools.partial(_ffn_kernel, tile=tile, d=d, n_chunks=nch),
        grid=(b, s // tile),
        in_specs=_halo_specs(tile, d, s) + [
            pl.BlockSpec((1, 1, mod.shape[2]), lambda bb, i: (bb, 0, 0)),
            _const_spec((1, d)), _const_spec((1, d)),
            _const_spec(wup.shape), _const_spec(cw.shape), _const_spec(cb.shape), _const_spec(wdn.shape)],
        out_specs=pl.BlockSpec((1, tile, d), lambda bb, i: (bb, i, 0)),
        out_shape=jax.ShapeDtypeStruct((b, s, d), F32),
        scratch_shapes=[pltpu.VMEM((tile + 2 * HALO, d), BF16),
                        pltpu.VMEM((tile + 2 * HALO, 2 * fc), F32),
                        pltpu.VMEM((tile, d), F32)],
        compiler_params=_cparams(2),
        name="convffn",
    )(h, h, h, mod, g_pre.reshape(1, d), g_post.reshape(1, d), wup, cw, cb, wdn)


CONF_ROWS = 32
CONF_COLS = 256


def _conformer_kernel(prev_ref, main_ref, next_ref, mod_ref, gpre_ref, gpost_ref, w1_ref, b1_ref,
                      dw_ref, db_ref, lg_ref, lb_ref, w2_ref, b2_ref, o_ref, hn_ref, u_ref, c_ref,
                      *, tile, d, seq):
    shift = mod_ref[0, :, 0:d]
    scale = mod_ref[0, :, d:2 * d]
    gate = mod_ref[0, :, 2 * d:3 * d]
    g = gpre_ref[...]
    _fill_halo_tile(hn_ref, prev_ref, main_ref, next_ref,
                    lambda t: _mod_norm(t, g, scale, shift), tile)
    ag = _dot(hn_ref[...], w1_ref[...]) + b1_ref[...]
    pos = (pl.program_id(1) * tile - HALO
           + lax.broadcasted_iota(jnp.int32, (tile + 2 * HALO, 1), 0))
    u_ref[...] = jnp.where((pos >= 0) & (pos < seq), ag[:, 0:d] * jax.nn.sigmoid(ag[:, d:2 * d]), 0.0)
    first = HALO - CONF_K // 2
    for r0 in range(0, tile, CONF_ROWS):
        for c0 in range(0, d, CONF_COLS):
            acc = jnp.zeros((CONF_ROWS, CONF_COLS), F32) + db_ref[:, c0:c0 + CONF_COLS]
            for k in range(CONF_K):
                acc = acc + dw_ref[k:k + 1, c0:c0 + CONF_COLS] * u_ref[pl.ds(r0 + first + k, CONF_ROWS),
                                                                      c0:c0 + CONF_COLS]
            c_ref[r0:r0 + CONF_ROWS, c0:c0 + CONF_COLS] = acc
    c = c_ref[...]
    mu = jnp.mean(c, axis=-1, keepdims=True)
    cc = c - mu
    var = jnp.mean(cc * cc, axis=-1, keepdims=True)
    ln = cc * lax.rsqrt(var + EPS) * lg_ref[...] + lb_ref[...]
    y = _dot(_silu(ln).astype(BF16), w2_ref[...]) + b2_ref[...]
    o_ref[0] = main_ref[0] + gate * _rms(y, gpost_ref[...])


def _conformer(h, mod, g_pre, g_post, w_pw1, b_pw1, w_dw, b_dw, ln_g, ln_b, w_pw2, b_pw2, tile):
    b, s, d = h.shape
    row = lambda t: t.reshape(1, -1)
    return pl.pallas_call(
        functools.partial(_conformer_kernel, tile=tile, d=d, seq=s),
        grid=(b, s // tile),
        in_specs=_halo_specs(tile, d, s) + [
            pl.BlockSpec((1, 1, mod.shape[2]), lambda bb, i: (bb, 0, 0)),
            _const_spec((1, d)), _const_spec((1, d)),
            _const_spec(w_pw1.shape), _const_spec((1, 2 * d)),
            _const_spec(w_dw.shape), _const_spec((1, d)), _const_spec((1, d)), _const_spec((1, d)),
            _const_spec(w_pw2.shape), _const_spec((1, d))],
        out_specs=pl.BlockSpec((1, tile, d), lambda bb, i: (bb, i, 0)),
        out_shape=jax.ShapeDtypeStruct((b, s, d), F32),
        scratch_shapes=[pltpu.VMEM((tile + 2 * HALO, d), BF16),
                        pltpu.VMEM((tile + 2 * HALO, d), F32),
                        pltpu.VMEM((tile, d), F32)],
        compiler_params=_cparams(2),
        name="conformer",
    )(h, h, h, mod, row(g_pre), row(g_post), w_pw1.astype(BF16), row(b_pw1), w_dw, row(b_dw),
      row(ln_g), row(ln_b), w_pw2.astype(BF16), row(b_pw2))


def _token_tile(seq):
    return min(seq, 512)


def kernel(x, c, ctx, c_ctx, w_mod, b_mod, g_mix_pre, g_mix_post, g_ffn_pre, g_ffn_post, w_in, w_out, hy_short_w, hy_short_b, hy_f_w1, hy_f_b1, hy_f_w2, hy_f_b2, hy_f_w3, hy_f_b3, hy_f_w4, hy_f_freq, hy_bias, na_rpb, cf_w_pw1, cf_b_pw1, cf_w_dw, cf_b_dw, cf_ln_g, cf_ln_b, cf_w_pw2, cf_b_pw2, ffn_w_up, ffn_w_dw, ffn_b_dw, ffn_w_down):
    bsz, seq, d = x.shape
    depth = w_mod.shape[0]
    hw = hy_bias.shape[1]
    aw = NA_HEADS * NA_HEAD_DIM
    tile = _token_tile(seq)
    assert seq % tile == 0 and seq % (GRID_W * NA_WIN_ROWS) == 0 and tile % HALO == 0

    n_rows = -(-(bsz + 1) // BF16_SUBLANES) * BF16_SUBLANES
    cc = jnp.concatenate([c, c_ctx[None, :], jnp.zeros((n_rows - bsz - 1, d), F32)], axis=0)
    mod_all = _adaln(cc, w_mod, b_mod)

    h = x
    for layer in range(depth):
        mod = mod_all[layer, :bsz].reshape(bsz, 1, 6 * d)
        if layer % 2 == 0:
            e = layer // 2
            w_e = w_in[e].astype(BF16)
            k_ctx, v_ctx = _ctxkv(ctx, mod_all[layer, bsz:bsz + 1, :2 * d], g_mix_pre[layer],
                                  w_e[:, 3 * hw + aw:])
            x0, xv, q, k, v = _inproj(h, mod, g_mix_pre[layer], w_e[:, :3 * hw], w_e[:, 3 * hw:],
                                      hy_short_w[e], hy_short_b[e], tile)
            f1, m2f, m2i, f3 = _dft_tables(seq)
            hf, hb = _hyena_filters(seq, hy_f_w1[e], hy_f_b1[e], hy_f_w2[e], hy_f_b2[e],
                                    hy_f_w3[e], hy_f_b3[e], hy_f_w4[e], hy_f_freq[e])
            kf = _filter_spectrum(hf, hb, f1, m2f)
            y_hy = _longconv(x0, xv, kf, hy_bias[e], f1, m2f, m2i, f3)
            y_na = _nattn(q, k, v, k_ctx, v_ctx, _bias_table(na_rpb[e]))
            h = _outproj(y_hy, y_na, h, mod, g_mix_post[layer], w_out[e].astype(BF16), tile)
        else:
            o = layer // 2
            h = _conformer(h, mod, g_mix_pre[layer], g_mix_post[layer], cf_w_pw1[o], cf_b_pw1[o],
                           cf_w_dw[o], cf_b_dw[o], cf_ln_g[o], cf_ln_b[o], cf_w_pw2[o], cf_b_pw2[o], tile)
        h = _ffn(h, mod, g_ffn_pre[layer], g_ffn_post[layer], ffn_w_up[layer], ffn_w_dw[layer],
                 ffn_b_dw[layer], ffn_w_down[layer], tile)
    return h
```

```python
import functools
import math

import numpy as np
import jax
import jax.numpy as jnp
from jax import lax
from jax.experimental import pallas as pl
from jax.experimental.pallas import tpu as pltpu

F32 = jnp.float32
BF16 = jnp.bfloat16

EPS = 1e-6
NEG_BIG = -1e30

GRID_W = 64
NA_HEADS = 8
NA_HEAD_DIM = 64
NA_WIN_ROWS = 8
NA_WIN_COLS = 16
HY_SHORT_K = 3
HY_EMB_DIM = 33
HY_DECAY_FAST = 0.3
HY_DECAY_SLOW = 1.5
HY_DECAY_TARGET = 1e-2
CONF_K = 31
FFN_CONV_K = 3

LANES = 128
F32_SUBLANES = 8
BF16_SUBLANES = 16
VMEM_LIMIT = 56 * 1024 * 1024

HALO = BF16_SUBLANES
DFT_N2 = 64
DFT_UNROLL = 16
NA_UNROLL = 2
FFN_CHUNK = 256
EMB_PAD = LANES


def _cparams(n_axes):
    return pltpu.CompilerParams(dimension_semantics=("arbitrary",) * n_axes,
                                vmem_limit_bytes=VMEM_LIMIT)


def _const_spec(shape):
    nd = len(shape)
    return pl.BlockSpec(shape, lambda *_: (0,) * nd, pipeline_mode=pl.Buffered(1))


def _silu(t):
    return t * jax.nn.sigmoid(t)


def _mod_norm(t, g, scale, shift):
    ms = jnp.mean(t * t, axis=-1, keepdims=True)
    return (t * lax.rsqrt(ms + EPS) * g) * (1.0 + scale) + shift


def _rms(t, g):
    ms = jnp.mean(t * t, axis=-1, keepdims=True)
    return t * lax.rsqrt(ms + EPS) * g


def _dot(a, b):
    return jnp.dot(a, b, preferred_element_type=F32)


def _dot_hi(a, b):
    return jnp.dot(a, b, preferred_element_type=F32, precision=lax.Precision.HIGHEST)


def _fill_halo_tile(hn_ref, prev_ref, main_ref, next_ref, norm_fn, tile):
    i = pl.program_id(1)
    last = pl.num_programs(1) - 1
    p = norm_fn(prev_ref[0])
    n = norm_fn(next_ref[0])
    hn_ref[0:HALO, :] = jnp.where(i > 0, p, 0.0).astype(BF16)
    hn_ref[HALO:HALO + tile, :] = norm_fn(main_ref[0]).astype(BF16)
    hn_ref[HALO + tile:, :] = jnp.where(i < last, n, 0.0).astype(BF16)


def _halo_specs(tile, d, seq):
    r = tile // HALO
    nblk = seq // HALO
    return [
        pl.BlockSpec((1, HALO, d), lambda b, i: (b, jnp.maximum(i * r - 1, 0), 0)),
        pl.BlockSpec((1, tile, d), lambda b, i: (b, i, 0)),
        pl.BlockSpec((1, HALO, d), lambda b, i: (b, jnp.minimum((i + 1) * r, nblk - 1), 0)),
    ]


def _adaln_kernel(c_ref, w_ref, b_ref, o_ref):
    s = _silu(c_ref[...]).astype(BF16)
    o_ref[0] = _dot(s, w_ref[0].astype(BF16)) + b_ref[0]


def _adaln(cc, w_mod, b_mod):
    depth, d, n6 = w_mod.shape
    r = cc.shape[0]
    nb = d
    return pl.pallas_call(
        _adaln_kernel,
        grid=(depth, n6 // nb),
        in_specs=[pl.BlockSpec((r, d), lambda l, j: (0, 0)),
                  pl.BlockSpec((1, d, nb), lambda l, j: (l, 0, j)),
                  pl.BlockSpec((1, 1, nb), lambda l, j: (l, 0, j))],
        out_specs=pl.BlockSpec((1, r, nb), lambda l, j: (l, 0, j)),
        out_shape=jax.ShapeDtypeStruct((depth, r, n6), F32),
        compiler_params=_cparams(2),
        name="adaln",
    )(cc, w_mod, b_mod.reshape(depth, 1, n6))


def _inproj_kernel(prev_ref, main_ref, next_ref, mod_ref, g_ref, why_ref, wqkv_ref, sw_ref, sb_ref,
                   x0_ref, xv_ref, q_ref, k_ref, v_ref, hn_ref, z_ref, *, tile, d, hw, aw):
    shift = mod_ref[0, :, 0:d]
    scale = mod_ref[0, :, d:2 * d]
    g = g_ref[...]
    _fill_halo_tile(hn_ref, prev_ref, main_ref, next_ref,
                    lambda t: _mod_norm(t, g, scale, shift), tile)
    z_ref[...] = _dot(hn_ref[...], why_ref[...])
    uc = (sw_ref[0:1, :] * z_ref[pl.ds(HALO - 1, tile), :]
          + sw_ref[1:2, :] * z_ref[pl.ds(HALO, tile), :]
          + sw_ref[2:3, :] * z_ref[pl.ds(HALO + 1, tile), :]
          + sb_ref[...])
    x0_ref[0] = uc[:, 0:hw]
    xv_ref[0] = uc[:, hw:2 * hw] * uc[:, 2 * hw:3 * hw]
    qkv = _dot(hn_ref[pl.ds(HALO, tile), :], wqkv_ref[...])
    q_ref[0] = (qkv[:, 0:aw] * (NA_HEAD_DIM ** -0.5)).astype(BF16)
    k_ref[0] = qkv[:, aw:2 * aw].astype(BF16)
    v_ref[0] = qkv[:, 2 * aw:3 * aw].astype(BF16)


def _inproj(h, mod, g, w_hy, w_qkv, short_w, short_b, tile):
    b, s, d = h.shape
    hw = w_hy.shape[1] // 3
    aw = w_qkv.shape[1] // 3
    tok = lambda width: pl.BlockSpec((1, tile, width), lambda bb, i: (bb, i, 0))
    return pl.pallas_call(
        functools.partial(_inproj_kernel, tile=tile, d=d, hw=hw, aw=aw),
        grid=(b, s // tile),
        in_specs=_halo_specs(tile, d, s) + [
            pl.BlockSpec((1, 1, mod.shape[2]), lambda bb, i: (bb, 0, 0)),
            _const_spec((1, d)),
            _const_spec(w_hy.shape),
            _const_spec(w_qkv.shape),
            _const_spec(short_w.shape),
            _const_spec((1, 3 * hw)),
        ],
        out_specs=[tok(hw), tok(hw), tok(aw), tok(aw), tok(aw)],
        out_shape=[jax.ShapeDtypeStruct((b, s, hw), F32), jax.ShapeDtypeStruct((b, s, hw), F32),
                   jax.ShapeDtypeStruct((b, s, aw), BF16), jax.ShapeDtypeStruct((b, s, aw), BF16),
                   jax.ShapeDtypeStruct((b, s, aw), BF16)],
        scratch_shapes=[pltpu.VMEM((tile + 2 * HALO, d), BF16),
                        pltpu.VMEM((tile + 2 * HALO, 3 * hw), F32)],
        compiler_params=_cparams(2),
        name="inproj",
    )(h, h, h, mod, g.reshape(1, d), w_hy, w_qkv, short_w, short_b.reshape(1, 3 * hw))


def _ctxkv_kernel(ctx_ref, mod_ref, g_ref, w_ref, k_ref, v_ref, *, d, aw):
    shift = mod_ref[:, 0:d]
    scale = mod_ref[:, d:2 * d]
    cn = _mod_norm(ctx_ref[0], g_ref[...], scale, shift).astype(BF16)
    kv = _dot(cn, w_ref[...])
    k_ref[0] = kv[:, 0:aw].astype(BF16)
    v_ref[0] = kv[:, aw:2 * aw].astype(BF16)


def _ctxkv(ctx, mod_ctx, g, w_kv):
    b, n, d = ctx.shape
    aw = w_kv.shape[1] // 2
    out = pl.BlockSpec((1, n, aw), lambda bb: (bb, 0, 0))
    return pl.pallas_call(
        functools.partial(_ctxkv_kernel, d=d, aw=aw),
        grid=(b,),
        in_specs=[pl.BlockSpec((1, n, d), lambda bb: (bb, 0, 0)),
                  _const_spec(mod_ctx.shape), _const_spec((1, d)), _const_spec(w_kv.shape)],
        out_specs=[out, out],
        out_shape=[jax.ShapeDtypeStruct((b, n, aw), BF16)] * 2,
        compiler_params=_cparams(1),
        name="ctxkv",
    )(ctx, mod_ctx, g.reshape(1, d), w_kv)


def _filter_kernel(z_ref, w1_ref, b1_ref, w2_ref, b2_ref, w3_ref, b3_ref, w4_ref, fr_ref, dl_ref,
                   hf_ref, hb_ref, *, rows, hw):
    z = z_ref[...]
    hdn = jnp.sin(fr_ref[0:1, :] * (_dot_hi(z, w1_ref[...]) + b1_ref[...]))
    hdn = jnp.sin(fr_ref[1:2, :] * (_dot_hi(hdn, w2_ref[...]) + b2_ref[...]))
    hdn = jnp.sin(fr_ref[2:3, :] * (_dot_hi(hdn, w3_ref[...]) + b3_ref[...]))
    filt = _dot_hi(hdn, w4_ref[...])
    decay = jnp.exp(-z[:, 0:1] * dl_ref[...])
    hf_ref[...] = filt[:, 0:hw] * decay
    pos = pl.program_id(0) * rows + lax.broadcasted_iota(jnp.int32, (rows, 1), 0)
    hb_ref[...] = jnp.where(pos > 0, filt[:, hw:2 * hw] * decay, 0.0)


def _hyena_filters(length, w1, b1, w2, b2, w3, b3, w4, freq):
    hid = w1.shape[1]
    hw = w4.shape[1] // 2
    bands = (HY_EMB_DIM - 1) // 2
    t01 = jnp.linspace(0.0, 1.0, length, dtype=F32)[:, None]
    w_pos = 2.0 * math.pi * jnp.arange(length, dtype=F32) / length
    f = jnp.linspace(1e-4, bands - 1, bands, dtype=F32)
    ang = w_pos[:, None] * f[None, :]
    z = jnp.concatenate([t01, jnp.cos(ang), -jnp.sin(ang),
                         jnp.zeros((length, EMB_PAD - HY_EMB_DIM), F32)], axis=-1)
    w1p = jnp.concatenate([w1, jnp.zeros((EMB_PAD - HY_EMB_DIM, hid), F32)], axis=0)
    deltas = jnp.abs(jnp.linspace(math.log(HY_DECAY_TARGET) / HY_DECAY_FAST,
                                  math.log(HY_DECAY_TARGET) / HY_DECAY_SLOW, hw, dtype=F32))[None, :]
    rows = min(length, 1024)
    out = pl.BlockSpec((rows, hw), lambda i: (i, 0))
    return pl.pallas_call(
        functools.partial(_filter_kernel, rows=rows, hw=hw),
        grid=(length // rows,),
        in_specs=[pl.BlockSpec((rows, EMB_PAD), lambda i: (i, 0)),
                  _const_spec(w1p.shape), _const_spec((1, hid)),
                  _const_spec(w2.shape), _const_spec((1, hid)),
                  _const_spec(w3.shape), _const_spec((1, hid)),
                  _const_spec(w4.shape), _const_spec(freq.shape), _const_spec(deltas.shape)],
        out_specs=[out, out],
        out_shape=[jax.ShapeDtypeStruct((length, hw), F32)] * 2,
        compiler_params=_cparams(1),
        name="hyena_filter",
    )(z, w1p, b1.reshape(1, hid), w2, b2.reshape(1, hid), w3, b3.reshape(1, hid), w4, freq, deltas)


def _dft_tables(seq):
    n2 = DFT_N2
    h1 = seq // n2
    n1 = 2 * h1
    n = n1 * n2
    k1 = np.arange(n1)[:, None]
    t1 = np.arange(h1)[None, :]
    a1 = 2.0 * np.pi * ((k1 * t1) % n1) / n1
    f1 = np.concatenate([np.cos(a1), -np.sin(a1)], axis=0)
    f3 = np.concatenate([np.cos(a1).T, -np.sin(a1).T], axis=1) / n
    kk1 = np.arange(n1)[:, None, None]
    k2 = np.arange(n2)[None, :, None]
    t2 = np.arange(n2)[None, None, :]
    th = 2.0 * np.pi * ((t2 * k2 * n1 + t2 * kk1) % n) / n
    wr, wi = np.cos(th), -np.sin(th)
    m2f = np.concatenate([np.concatenate([wr, -wi], axis=2),
                          np.concatenate([wi, wr], axis=2)], axis=1)
    m2i = np.transpose(m2f, (0, 2, 1))
    as_bf16 = lambda a: jnp.asarray(a, F32).astype(BF16)
    return as_bf16(f1), as_bf16(m2f), as_bf16(m2i), as_bf16(f3)


def _pitch(rows):
    return rows + F32_SUBLANES


def _dft_stage1(load_rows, f1_ref, a_ref, n1):
    def body(t2, carry):
        p = _dot(f1_ref[...], load_rows(t2).astype(BF16))
        a_ref[pl.ds(pl.multiple_of(t2 * _pitch(2 * n1), F32_SUBLANES), 2 * n1), :] = p
        return carry
    lax.fori_loop(0, DFT_N2, body, 0, unroll=DFT_UNROLL)


def _dft_stage2(a_ref, m2f_ref, k1, n1):
    ar = a_ref[pl.ds(k1, DFT_N2, stride=_pitch(2 * n1)), :]
    ai = a_ref[pl.ds(n1 + k1, DFT_N2, stride=_pitch(2 * n1)), :]
    return _dot(m2f_ref[k1], jnp.concatenate([ar, ai], axis=0).astype(BF16))


def _spectrum_kernel(hf_ref, hb_ref, f1_ref, m2f_ref, kf_ref, af_ref, ab_ref, *, n1):
    n2 = DFT_N2
    _dft_stage1(lambda t2: hf_ref[pl.ds(t2, n1 // 2, stride=n2), :], f1_ref, af_ref, n1)
    _dft_stage1(lambda t2: hb_ref[pl.ds(t2, n1 // 2, stride=n2), :], f1_ref, ab_ref, n1)

    def body(k1, carry):
        xf = _dft_stage2(af_ref, m2f_ref, k1, n1)
        xb = _dft_stage2(ab_ref, m2f_ref, k1, n1)
        sign = jnp.where(lax.broadcasted_iota(jnp.int32, (2 * n2, 1), 0) < n2, 1.0, -1.0)
        kf_ref[k1] = xf + sign * xb
        return carry
    lax.fori_loop(0, n1, body, 0, unroll=DFT_UNROLL)


def _filter_spectrum(hf, hb, f1, m2f):
    s, hw = hf.shape
    n2 = DFT_N2
    n1 = 2 * s // n2
    cb = LANES
    col = pl.BlockSpec((s, cb), lambda j: (0, j))
    return pl.pallas_call(
        functools.partial(_spectrum_kernel, n1=n1),
        grid=(hw // cb,),
        in_specs=[col, col, _const_spec(f1.shape), _const_spec(m2f.shape)],
        out_specs=pl.BlockSpec((n1, 2 * n2, cb), lambda j: (0, 0, j)),
        out_shape=jax.ShapeDtypeStruct((n1, 2 * n2, hw), F32),
        scratch_shapes=[pltpu.VMEM((n2 * _pitch(2 * n1), cb), F32), pltpu.VMEM((n2 * _pitch(2 * n1), cb), F32)],
        compiler_params=_cparams(1),
        name="filter_spectrum",
    )(hf, hb, f1, m2f)


def _longconv_kernel(x0_ref, xv_ref, kf_ref, bias_ref, f1_ref, m2f_ref, m2i_ref, f3_ref, o_ref,
                     a_ref, b_ref, *, n1):
    n2 = DFT_N2
    h1 = n1 // 2
    _dft_stage1(lambda t2: xv_ref[0, pl.ds(t2, h1, stride=n2), :], f1_ref, a_ref, n1)

    def per_k1(k1, carry):
        x = _dft_stage2(a_ref, m2f_ref, k1, n1)
        kf = kf_ref[k1]
        xr, xi = x[0:n2], x[n2:2 * n2]
        kr, ki = kf[0:n2], kf[n2:2 * n2]
        y = jnp.concatenate([xr * kr - xi * ki, xr * ki + xi * kr], axis=0).astype(BF16)
        b_ref[pl.ds(pl.multiple_of(k1 * _pitch(2 * n2), F32_SUBLANES), 2 * n2), :] = _dot(m2i_ref[k1], y)
        return carry
    lax.fori_loop(0, n1, per_k1, 0, unroll=DFT_UNROLL)

    def per_t2(t2, carry):
        br = b_ref[pl.ds(t2, n1, stride=_pitch(2 * n2)), :]
        bi = b_ref[pl.ds(n2 + t2, n1, stride=_pitch(2 * n2)), :]
        y = _dot(f3_ref[...], jnp.concatenate([br, bi], axis=0).astype(BF16))
        rows = pl.ds(t2, h1, stride=n2)
        o_ref[0, rows, :] = x0_ref[0, rows, :] * (y + xv_ref[0, rows, :] * bias_ref[...])
        return carry
    lax.fori_loop(0, n2, per_t2, 0, unroll=DFT_UNROLL)


def _longconv(x0, xv, kf, bias, f1, m2f, m2i, f3):
    b, s, hw = x0.shape
    n2 = DFT_N2
    n1 = 2 * s // n2
    cb = LANES
    tok = pl.BlockSpec((1, s, cb), lambda j, bb: (bb, 0, j))
    return pl.pallas_call(
        functools.partial(_longconv_kernel, n1=n1),
        grid=(hw // cb, b),
        in_specs=[tok, tok,
                  pl.BlockSpec((n1, 2 * n2, cb), lambda j, bb: (0, 0, j), pipeline_mode=pl.Buffered(1)),
                  pl.BlockSpec((1, cb), lambda j, bb: (0, j)),
                  _const_spec(f1.shape), _const_spec(m2f.shape), _const_spec(m2i.shape),
                  _const_spec(f3.shape)],
        out_specs=tok,
        out_shape=jax.ShapeDtypeStruct((b, s, hw), F32),
        scratch_shapes=[pltpu.VMEM((n2 * _pitch(2 * n1), cb), F32), pltpu.VMEM((n1 * _pitch(2 * n2), cb), F32)],
        compiler_params=_cparams(2),
        name="longconv",
    )(x0, xv, kf, bias.reshape(1, hw), f1, m2f, m2i, f3)


NA_QROWS = NA_WIN_ROWS // 2
NA_BAND_ROWS = NA_QROWS + NA_WIN_ROWS


def _bias_kernel(rpb_ref, o_ref, *, n_dc):
    h = pl.program_id(0)
    n_dr = 2 * NA_WIN_ROWS - 1
    qcol = lax.broadcasted_iota(jnp.int32, (GRID_W, LANES), 0)
    lane = lax.broadcasted_iota(jnp.int32, (GRID_W, LANES), 1)
    kcol = lane % GRID_W
    win = jnp.clip(qcol - NA_WIN_COLS // 2, 0, GRID_W - NA_WIN_COLS)
    col_ok = (kcol >= win) & (kcol < win + NA_WIN_COLS)
    low_half = lane < GRID_W
    lane_row = lax.broadcasted_iota(jnp.int32, (1, LANES), 1)
    toeplitz = []
    for dr in range(n_dr):
        vec = jnp.zeros((1, LANES), F32)
        for d in range(n_dc):
            vec = jnp.where(lane_row == d, rpb_ref[(h * n_dr + dr) * n_dc + d], vec)
        rows = jnp.broadcast_to(vec, (GRID_W, LANES))
        toeplitz.append([pltpu.roll(rows, (LANES - (NA_WIN_COLS - 1) + half * GRID_W) % LANES, 1,
                                    stride=1, stride_axis=0) for half in range(2)])
    neg = jnp.full((GRID_W, LANES), NEG_BIG, F32)
    for pos in range(3):
        for p in range(NA_QROWS):
            lo = (0, p, NA_QROWS)[pos]
            for jj in range(NA_BAND_ROWS // 2):
                halves = []
                for half in range(2):
                    j = 2 * jj + half
                    dr = j - p + NA_WIN_ROWS - 1 - NA_QROWS * pos
                    halves.append(toeplitz[dr][half] if lo <= j < lo + NA_WIN_ROWS else neg)
                tile = jnp.where(low_half, halves[0], halves[1])
                o_ref[pos, 0, p * GRID_W:(p + 1) * GRID_W, jj * LANES:(jj + 1) * LANES] = (
                    jnp.where(col_ok, tile, NEG_BIG))


def _bias_table(rpb):
    heads, n_dr, n_dc = rpb.shape
    blk = (NA_QROWS * GRID_W, NA_BAND_ROWS * GRID_W)
    assert 2 * GRID_W == LANES and NA_BAND_ROWS % 2 == 0 and n_dc <= LANES
    return pl.pallas_call(
        functools.partial(_bias_kernel, n_dc=n_dc),
        grid=(heads,),
        in_specs=[pl.BlockSpec(memory_space=pltpu.SMEM)],
        out_specs=pl.BlockSpec((3, 1) + blk, lambda h: (0, h, 0, 0)),
        out_shape=jax.ShapeDtypeStruct((3, heads) + blk, F32),
        compiler_params=_cparams(1),
        name="na_bias",
    )(rpb.reshape(-1))


def _nattn_kernel(q_ref, k_ref, v_ref, kc_ref, vc_ref, bias_ref, o_ref, vs_ref, vcs_ref, *, rows):
    dh = NA_HEAD_DIM
    nq = NA_QROWS * GRID_W
    band = NA_BAND_ROWS * GRID_W
    nt = (((1,), (1,)), ((), ()))

    def own_lanes(n_rows):
        lane = lax.broadcasted_iota(jnp.int32, (n_rows, LANES), 1)
        return [lane < dh, lane >= dh]

    for hh in range(2):
        vs_ref[hh] = jnp.where(own_lanes(v_ref.shape[1])[hh], v_ref[0], 1.0).astype(BF16)
        vcs_ref[hh] = jnp.where(own_lanes(vc_ref.shape[1])[hh], vc_ref[0], 1.0).astype(BF16)
    own_q = own_lanes(nq)
    n_blocks = rows // NA_QROWS

    def per_block(i, carry):
        r0 = i * NA_QROWS
        pos = jnp.where(i == 0, 0, jnp.where(i == n_blocks - 1, 2, 1))
        key0 = pl.multiple_of(jnp.clip(r0 - NA_WIN_ROWS // 2, 0, rows - NA_BAND_ROWS) * GRID_W, GRID_W)
        qrows = pl.ds(pl.multiple_of(r0 * GRID_W, nq), nq)
        q = q_ref[0, qrows, :]
        kb = k_ref[0, pl.ds(key0, band), :]
        outs = []
        for hh in range(2):
            qh = jnp.where(own_q[hh], q, 0.0).astype(BF16)
            s_loc = lax.dot_general(qh, kb, nt, preferred_element_type=F32) + bias_ref[pos, hh]
            s_ctx = lax.dot_general(qh, kc_ref[0], nt, preferred_element_type=F32)
            s = jnp.concatenate([s_loc, s_ctx], axis=-1)
            p = jnp.exp(s - jnp.max(s, axis=-1, keepdims=True)).astype(BF16)
            o = _dot(p[:, 0:band], vs_ref[hh, pl.ds(key0, band), :]) + _dot(p[:, band:], vcs_ref[hh])
            outs.append(o / pltpu.roll(o, dh, axis=1))
        o_ref[0, qrows, :] = jnp.where(own_q[0], outs[0], outs[1]).astype(BF16)
        return carry
    lax.fori_loop(0, n_blocks, per_block, 0, unroll=NA_UNROLL)


def _nattn(q, k, v, kc, vc, bias):
    b, s, aw = q.shape
    nctx = kc.shape[1]
    hps = LANES // NA_HEAD_DIM
    assert (s // GRID_W) % NA_QROWS == 0 and s // GRID_W >= NA_BAND_ROWS
    tok = pl.BlockSpec((1, s, LANES), lambda j, bb: (bb, 0, j))
    ctx = pl.BlockSpec((1, nctx, LANES), lambda j, bb: (bb, 0, j))
    return pl.pallas_call(
        functools.partial(_nattn_kernel, rows=s // GRID_W),
        grid=(aw // LANES, b),
        in_specs=[tok, tok, tok, ctx, ctx,
                  pl.BlockSpec((3, hps) + bias.shape[2:], lambda j, bb: (0, j, 0, 0))],
        out_specs=tok,
        out_shape=jax.ShapeDtypeStruct((b, s, aw), BF16),
        scratch_shapes=[pltpu.VMEM((hps, s, LANES), BF16), pltpu.VMEM((hps, nctx, LANES), BF16)],
        compiler_params=_cparams(2),
        name="nattn",
    )(q, k, v, kc, vc, bias)


def _outproj_kernel(yh_ref, ya_ref, h_ref, mod_ref, g_ref, w_ref, o_ref, *, d, hw):
    gate = mod_ref[0, :, 2 * d:3 * d]
    y = _dot(yh_ref[0].astype(BF16), w_ref[0:hw, :]) + _dot(ya_ref[0], w_ref[hw:, :])
    o_ref[0] = h_ref[0] + gate * _rms(y, g_ref[...])


def _outproj(y_hy, y_na, h, mod, g, w_out, tile):
    b, s, d = h.shape
    hw = y_hy.shape[2]
    aw = y_na.shape[2]
    tok = lambda width: pl.BlockSpec((1, tile, width), lambda bb, i: (bb, i, 0))
    return pl.pallas_call(
        functools.partial(_outproj_kernel, d=d, hw=hw),
        grid=(b, s // tile),
        in_specs=[tok(hw), tok(aw), tok(d),
                  pl.BlockSpec((1, 1, mod.shape[2]), lambda bb, i: (bb, 0, 0)),
                  _const_spec((1, d)), _const_spec(w_out.shape)],
        out_specs=tok(d),
        out_shape=jax.ShapeDtypeStruct((b, s, d), F32),
        compiler_params=_cparams(2),
        name="outproj",
    )(y_hy, y_na, h, mod, g.reshape(1, d), w_out)


def _ffn_kernel(prev_ref, main_ref, next_ref, mod_ref, gpre_ref, gpost_ref, wup_ref, cw_ref, cb_ref, wdn_ref,
                o_ref, hn_ref, z_ref, acc_ref, *, tile, d, n_chunks):
    fc = FFN_CHUNK
    shift = mod_ref[0, :, 3 * d:4 * d]
    scale = mod_ref[0, :, 4 * d:5 * d]
    gate = mod_ref[0, :, 5 * d:6 * d]
    g = gpre_ref[...]
    _fill_halo_tile(hn_ref, prev_ref, main_ref, next_ref,
                    lambda t: _mod_norm(t, g, scale, shift), tile)
    acc_ref[...] = jnp.zeros_like(acc_ref)

    def up(j, slot):
        z_ref[slot] = _dot(hn_ref[...], wup_ref[j])

    def down(j, slot):
        cw = cw_ref[j]
        c = (cw[0:1, :] * z_ref[slot, pl.ds(HALO - 1, tile), :]
             + cw[1:2, :] * z_ref[slot, pl.ds(HALO, tile), :]
             + cw[2:3, :] * z_ref[slot, pl.ds(HALO + 1, tile), :]
             + cb_ref[j])
        a = (_silu(c[:, 0:fc]) * c[:, fc:2 * fc]).astype(BF16)
        acc_ref[...] += _dot(a, wdn_ref[j])

    def step(j, slot):
        up(j + 1, 1 - slot)
        down(j, slot)

    up(0, 0)
    n_pairs = (n_chunks - 1) // 2

    def pair(jj, carry):
        step(2 * jj, 0)
        step(2 * jj + 1, 1)
        return carry
    lax.fori_loop(0, n_pairs, pair, 0)
    if (n_chunks - 1) % 2:
        step(n_chunks - 2, 0)
    down(n_chunks - 1, (n_chunks - 1) % 2)
    o_ref[0] = main_ref[0] + gate * _rms(acc_ref[...], gpost_ref[...])


def _ffn(h, mod, g_pre, g_post, w_up, w_dw, b_dw, w_down, tile):
    b, s, d = h.shape
    dff = w_down.shape[0]
    fc = FFN_CHUNK
    nch = dff // fc
    pair = lambda t: jnp.concatenate([t[..., :dff].reshape(t.shape[:-1] + (nch, fc)),
                                      t[..., dff:].reshape(t.shape[:-1] + (nch, fc))], axis=-1)
    wup = jnp.transpose(pair(w_up), (1, 0, 2)).astype(BF16)
    cw = jnp.transpose(pair(w_dw), (1, 0, 2))
    cb = pair(b_dw).reshape(nch, 1, 2 * fc)
    wdn = w_down.reshape(nch, fc, d).astype(BF16)
    return pl.pallas_call(
        functools.partial(_ffn_kernel, tile=tile, d=d, n_chunks=nch),
        grid=(b, s // tile),
        in_specs=_halo_specs(tile, d, s) + [
            pl.BlockSpec((1, 1, mod.shape[2]), lambda bb, i: (bb, 0, 0)),
            _const_spec((1, d)), _const_spec((1, d)),
            _const_spec(wup.shape), _const_spec(cw.shape), _const_spec(cb.shape), _const_spec(wdn.shape)],
        out_specs=pl.BlockSpec((1, tile, d), lambda bb, i: (bb, i, 0)),
        out_shape=jax.ShapeDtypeStruct((b, s, d), F32),
        scratch_shapes=[pltpu.VMEM((tile + 2 * HALO, d), BF16),
                        pltpu.VMEM((2, tile + 2 * HALO, 2 * fc), F32),
                        pltpu.VMEM((tile, d), F32)],
        compiler_params=_cparams(2),
        name="convffn",
    )(h, h, h, mod, g_pre.reshape(1, d), g_post.reshape(1, d), wup, cw, cb, wdn)


CONF_ROWS = 64
CONF_COLS = LANES


def _conformer_kernel(prev_ref, main_ref, next_ref, mod_ref, gpre_ref, gpost_ref, w1_ref, b1_ref,
                      dw_ref, db_ref, lg_ref, lb_ref, w2_ref, b2_ref, o_ref, hn_ref, u_ref, c_ref,
                      *, tile, d, seq):
    shift = mod_ref[0, :, 0:d]
    scale = mod_ref[0, :, d:2 * d]
    gate = mod_ref[0, :, 2 * d:3 * d]
    g = gpre_ref[...]
    _fill_halo_tile(hn_ref, prev_ref, main_ref, next_ref,
                    lambda t: _mod_norm(t, g, scale, shift), tile)
    ag = _dot(hn_ref[...], w1_ref[...]) + b1_ref[...]
    pos = (pl.program_id(1) * tile - HALO
           + lax.broadcasted_iota(jnp.int32, (tile + 2 * HALO, 1), 0))
    u_ref[...] = jnp.where((pos >= 0) & (pos < seq), ag[:, 0:d] * jax.nn.sigmoid(ag[:, d:2 * d]), 0.0)
    first = HALO - CONF_K // 2
    reach = -(-(first + CONF_K - 1) // F32_SUBLANES)

    def conv_block(rb, carry):
        r0 = pl.multiple_of(rb * CONF_ROWS, CONF_ROWS)
        for c0 in range(0, d, CONF_COLS):
            cols = slice(c0, c0 + CONF_COLS)
            slab = u_ref[pl.ds(r0, CONF_ROWS + 2 * HALO), cols]
            acc = jnp.broadcast_to(db_ref[:, cols], (CONF_ROWS, CONF_COLS))
            n_slab = CONF_ROWS + 2 * HALO
            for s in range(F32_SUBLANES):
                shifted = pltpu.roll(slab, (n_slab - s) % n_slab, 0) if s else slab
                for a in range(reach):
                    k = a * F32_SUBLANES + s - first
                    if 0 <= k < CONF_K:
                        acc = acc + dw_ref[k:k + 1, cols] * shifted[a * F32_SUBLANES:a * F32_SUBLANES + CONF_ROWS]
            c_ref[pl.ds(r0, CONF_ROWS), cols] = acc
        return carry
    lax.fori_loop(0, tile // CONF_ROWS, conv_block, 0)
    c = c_ref[...]
    mu = jnp.mean(c, axis=-1, keepdims=True)
    cc = c - mu
    var = jnp.mean(cc * cc, axis=-1, keepdims=True)
    ln = cc * lax.rsqrt(var + EPS) * lg_ref[...] + lb_ref[...]
    y = _dot(_silu(ln).astype(BF16), w2_ref[...]) + b2_ref[...]
    o_ref[0] = main_ref[0] + gate * _rms(y, gpost_ref[...])


def _conformer(h, mod, g_pre, g_post, w_pw1, b_pw1, w_dw, b_dw, ln_g, ln_b, w_pw2, b_pw2, tile):
    b, s, d = h.shape
    row = lambda t: t.reshape(1, -1)
    return pl.pallas_call(
        functools.partial(_conformer_kernel, tile=tile, d=d, seq=s),
        grid=(b, s // tile),
        in_specs=_halo_specs(tile, d, s) + [
            pl.BlockSpec((1, 1, mod.shape[2]), lambda bb, i: (bb, 0, 0)),
            _const_spec((1, d)), _const_spec((1, d)),
            _const_spec(w_pw1.shape), _const_spec((1, 2 * d)),
            _const_spec(w_dw.shape), _const_spec((1, d)), _const_spec((1, d)), _const_spec((1, d)),
            _const_spec(w_pw2.shape), _const_spec((1, d))],
        out_specs=pl.BlockSpec((1, tile, d), lambda bb, i: (bb, i, 0)),
        out_shape=jax.ShapeDtypeStruct((b, s, d), F32),
        scratch_shapes=[pltpu.VMEM((tile + 2 * HALO, d), BF16),
                        pltpu.VMEM((tile + 2 * HALO, d), F32),
                        pltpu.VMEM((tile, d), F32)],
        compiler_params=_cparams(2),
        name="conformer",
    )(h, h, h, mod, row(g_pre), row(g_post), w_pw1.astype(BF16), row(b_pw1), w_dw, row(b_dw),
      row(ln_g), row(ln_b), w_pw2.astype(BF16), row(b_pw2))


def _token_tile(seq):
    return min(seq, 512)


def kernel(x, c, ctx, c_ctx, w_mod, b_mod, g_mix_pre, g_mix_post, g_ffn_pre, g_ffn_post, w_in, w_out, hy_short_w, hy_short_b, hy_f_w1, hy_f_b1, hy_f_w2, hy_f_b2, hy_f_w3, hy_f_b3, hy_f_w4, hy_f_freq, hy_bias, na_rpb, cf_w_pw1, cf_b_pw1, cf_w_dw, cf_b_dw, cf_ln_g, cf_ln_b, cf_w_pw2, cf_b_pw2, ffn_w_up, ffn_w_dw, ffn_b_dw, ffn_w_down):
    bsz, seq, d = x.shape
    depth = w_mod.shape[0]
    hw = hy_bias.shape[1]
    aw = NA_HEADS * NA_HEAD_DIM
    tile = _token_tile(seq)
    assert seq % tile == 0 and seq % (GRID_W * NA_WIN_ROWS) == 0 and tile % HALO == 0

    n_rows = -(-(bsz + 1) // BF16_SUBLANES) * BF16_SUBLANES
    cc = jnp.concatenate([c, c_ctx[None, :], jnp.zeros((n_rows - bsz - 1, d), F32)], axis=0)
    mod_all = _adaln(cc, w_mod, b_mod)

    h = x
    for layer in range(depth):
        mod = mod_all[layer, :bsz].reshape(bsz, 1, 6 * d)
        if layer % 2 == 0:
            e = layer // 2
            w_e = w_in[e].astype(BF16)
            k_ctx, v_ctx = _ctxkv(ctx, mod_all[layer, bsz:bsz + 1, :2 * d], g_mix_pre[layer],
                                  w_e[:, 3 * hw + aw:])
            x0, xv, q, k, v = _inproj(h, mod, g_mix_pre[layer], w_e[:, :3 * hw], w_e[:, 3 * hw:],
                                      hy_short_w[e], hy_short_b[e], tile)
            f1, m2f, m2i, f3 = _dft_tables(seq)
            hf, hb = _hyena_filters(seq, hy_f_w1[e], hy_f_b1[e], hy_f_w2[e], hy_f_b2[e],
                                    hy_f_w3[e], hy_f_b3[e], hy_f_w4[e], hy_f_freq[e])
            kf = _filter_spectrum(hf, hb, f1, m2f)
            y_hy = _longconv(x0, xv, kf, hy_bias[e], f1, m2f, m2i, f3)
            y_na = _nattn(q, k, v, k_ctx, v_ctx, _bias_table(na_rpb[e]))
            h = _outproj(y_hy, y_na, h, mod, g_mix_post[layer], w_out[e].astype(BF16), tile)
        else:
            o = layer // 2
            h = _conformer(h, mod, g_mix_pre[layer], g_mix_post[layer], cf_w_pw1[o], cf_b_pw1[o],
                           cf_w_dw[o], cf_b_dw[o], cf_ln_g[o], cf_ln_b[o], cf_w_pw2[o], cf_b_pw2[o], tile)
        h = _ffn(h, mod, g_ffn_pre[layer], g_ffn_post[layer], ffn_w_up[layer], ffn_w_dw[layer],
                 ffn_b_dw[layer], ffn_w_down[layer], tile)
    return h
```

```python
import functools
import math

import numpy as np
import jax
import jax.numpy as jnp
from jax import lax
from jax.experimental import pallas as pl
from jax.experimental.pallas import tpu as pltpu

F32 = jnp.float32
BF16 = jnp.bfloat16

EPS = 1e-6
NEG_BIG = -1e30
LOG2_E = math.log2(math.e)

GRID_W = 64
NA_HEADS = 8
NA_HEAD_DIM = 64
NA_WIN_ROWS = 8
NA_WIN_COLS = 16
HY_SHORT_K = 3
HY_EMB_DIM = 33
HY_DECAY_FAST = 0.3
HY_DECAY_SLOW = 1.5
HY_DECAY_TARGET = 1e-2
CONF_K = 31
FFN_CONV_K = 3

LANES = 128
F32_SUBLANES = 8
BF16_SUBLANES = 16
VMEM_LIMIT = 56 * 1024 * 1024

HALO = BF16_SUBLANES
DFT_N2 = 64
DFT_UNROLL = 16
NA_UNROLL = 2
FFN_CHUNK = 256
EMB_PAD = LANES


def _cparams(n_axes):
    return pltpu.CompilerParams(dimension_semantics=("arbitrary",) * n_axes,
                                vmem_limit_bytes=VMEM_LIMIT)


def _const_spec(shape):
    nd = len(shape)
    return pl.BlockSpec(shape, lambda *_: (0,) * nd, pipeline_mode=pl.Buffered(1))


def _silu(t):
    return t * jax.nn.sigmoid(t)


def _mod_norm(t, g, scale, shift):
    ms = jnp.mean(t * t, axis=-1, keepdims=True)
    return (t * lax.rsqrt(ms + EPS) * g) * (1.0 + scale) + shift


def _rms(t, g):
    ms = jnp.mean(t * t, axis=-1, keepdims=True)
    return t * lax.rsqrt(ms + EPS) * g


def _dot(a, b):
    return jnp.dot(a, b, preferred_element_type=F32)


def _dot_hi(a, b):
    return jnp.dot(a, b, preferred_element_type=F32, precision=lax.Precision.HIGHEST)


def _fill_halo_tile(hn_ref, prev_ref, main_ref, next_ref, norm_fn, tile):
    i = pl.program_id(1)
    last = pl.num_programs(1) - 1
    p = norm_fn(prev_ref[0])
    n = norm_fn(next_ref[0])
    hn_ref[0:HALO, :] = jnp.where(i > 0, p, 0.0).astype(BF16)
    hn_ref[HALO:HALO + tile, :] = norm_fn(main_ref[0]).astype(BF16)
    hn_ref[HALO + tile:, :] = jnp.where(i < last, n, 0.0).astype(BF16)


def _halo_specs(tile, d, seq):
    r = tile // HALO
    nblk = seq // HALO
    return [
        pl.BlockSpec((1, HALO, d), lambda b, i: (b, jnp.maximum(i * r - 1, 0), 0)),
        pl.BlockSpec((1, tile, d), lambda b, i: (b, i, 0)),
        pl.BlockSpec((1, HALO, d), lambda b, i: (b, jnp.minimum((i + 1) * r, nblk - 1), 0)),
    ]


def _adaln_kernel(c_ref, w_ref, b_ref, o_ref):
    s = _silu(c_ref[...]).astype(BF16)
    o_ref[0] = _dot(s, w_ref[0].astype(BF16)) + b_ref[0]


def _adaln(cc, w_mod, b_mod):
    depth, d, n6 = w_mod.shape
    r = cc.shape[0]
    nb = d
    return pl.pallas_call(
        _adaln_kernel,
        grid=(depth, n6 // nb),
        in_specs=[pl.BlockSpec((r, d), lambda l, j: (0, 0)),
                  pl.BlockSpec((1, d, nb), lambda l, j: (l, 0, j)),
                  pl.BlockSpec((1, 1, nb), lambda l, j: (l, 0, j))],
        out_specs=pl.BlockSpec((1, r, nb), lambda l, j: (l, 0, j)),
        out_shape=jax.ShapeDtypeStruct((depth, r, n6), F32),
        compiler_params=_cparams(2),
        name="adaln",
    )(cc, w_mod, b_mod.reshape(depth, 1, n6))


def _inproj_kernel(prev_ref, main_ref, next_ref, mod_ref, g_ref, why_ref, wqkv_ref, sw_ref, sb_ref,
                   x0_ref, xv_ref, q_ref, k_ref, v_ref, hn_ref, z_ref, *, tile, d, hw, aw):
    shift = mod_ref[0, :, 0:d]
    scale = mod_ref[0, :, d:2 * d]
    g = g_ref[...]
    _fill_halo_tile(hn_ref, prev_ref, main_ref, next_ref,
                    lambda t: _mod_norm(t, g, scale, shift), tile)
    z_ref[...] = _dot(hn_ref[...], why_ref[...])
    uc = (sw_ref[0:1, :] * z_ref[pl.ds(HALO - 1, tile), :]
          + sw_ref[1:2, :] * z_ref[pl.ds(HALO, tile), :]
          + sw_ref[2:3, :] * z_ref[pl.ds(HALO + 1, tile), :]
          + sb_ref[...])
    x0_ref[0] = uc[:, 0:hw]
    xv_ref[0] = uc[:, hw:2 * hw] * uc[:, 2 * hw:3 * hw]
    qkv = _dot(hn_ref[pl.ds(HALO, tile), :], wqkv_ref[...])
    q_ref[0] = (qkv[:, 0:aw] * (NA_HEAD_DIM ** -0.5 * LOG2_E)).astype(BF16)
    k_ref[0] = qkv[:, aw:2 * aw].astype(BF16)
    v_ref[0] = qkv[:, 2 * aw:3 * aw].astype(BF16)


def _inproj(h, mod, g, w_hy, w_qkv, short_w, short_b, tile):
    b, s, d = h.shape
    hw = w_hy.shape[1] // 3
    aw = w_qkv.shape[1] // 3
    tok = lambda width: pl.BlockSpec((1, tile, width), lambda bb, i: (bb, i, 0))
    return pl.pallas_call(
        functools.partial(_inproj_kernel, tile=tile, d=d, hw=hw, aw=aw),
        grid=(b, s // tile),
        in_specs=_halo_specs(tile, d, s) + [
            pl.BlockSpec((1, 1, mod.shape[2]), lambda bb, i: (bb, 0, 0)),
            _const_spec((1, d)),
            _const_spec(w_hy.shape),
            _const_spec(w_qkv.shape),
            _const_spec(short_w.shape),
            _const_spec((1, 3 * hw)),
        ],
        out_specs=[tok(hw), tok(hw), tok(aw), tok(aw), tok(aw)],
        out_shape=[jax.ShapeDtypeStruct((b, s, hw), F32), jax.ShapeDtypeStruct((b, s, hw), F32),
                   jax.ShapeDtypeStruct((b, s, aw), BF16), jax.ShapeDtypeStruct((b, s, aw), BF16),
                   jax.ShapeDtypeStruct((b, s, aw), BF16)],
        scratch_shapes=[pltpu.VMEM((tile + 2 * HALO, d), BF16),
                        pltpu.VMEM((tile + 2 * HALO, 3 * hw), F32)],
        compiler_params=_cparams(2),
        name="inproj",
    )(h, h, h, mod, g.reshape(1, d), w_hy, w_qkv, short_w, short_b.reshape(1, 3 * hw))


def _ctxkv_kernel(ctx_ref, mod_ref, g_ref, w_ref, k_ref, v_ref, *, d, aw):
    shift = mod_ref[:, 0:d]
    scale = mod_ref[:, d:2 * d]
    cn = _mod_norm(ctx_ref[0], g_ref[...], scale, shift).astype(BF16)
    kv = _dot(cn, w_ref[...])
    k_ref[0] = kv[:, 0:aw].astype(BF16)
    v_ref[0] = kv[:, aw:2 * aw].astype(BF16)


def _ctxkv(ctx, mod_ctx, g, w_kv):
    b, n, d = ctx.shape
    aw = w_kv.shape[1] // 2
    out = pl.BlockSpec((1, n, aw), lambda bb: (bb, 0, 0))
    return pl.pallas_call(
        functools.partial(_ctxkv_kernel, d=d, aw=aw),
        grid=(b,),
        in_specs=[pl.BlockSpec((1, n, d), lambda bb: (bb, 0, 0)),
                  _const_spec(mod_ctx.shape), _const_spec((1, d)), _const_spec(w_kv.shape)],
        out_specs=[out, out],
        out_shape=[jax.ShapeDtypeStruct((b, n, aw), BF16)] * 2,
        compiler_params=_cparams(1),
        name="ctxkv",
    )(ctx, mod_ctx, g.reshape(1, d), w_kv)


def _filter_kernel(z_ref, w1_ref, b1_ref, w2_ref, b2_ref, w3_ref, b3_ref, w4_ref, fr_ref, dl_ref,
                   hf_ref, hb_ref, *, rows, hw):
    z = z_ref[...]
    hdn = jnp.sin(fr_ref[0:1, :] * (_dot_hi(z, w1_ref[...]) + b1_ref[...]))
    hdn = jnp.sin(fr_ref[1:2, :] * (_dot_hi(hdn, w2_ref[...]) + b2_ref[...]))
    hdn = jnp.sin(fr_ref[2:3, :] * (_dot_hi(hdn, w3_ref[...]) + b3_ref[...]))
    filt = _dot_hi(hdn, w4_ref[...])
    decay = jnp.exp(-z[:, 0:1] * dl_ref[...])
    hf_ref[...] = filt[:, 0:hw] * decay
    pos = pl.program_id(0) * rows + lax.broadcasted_iota(jnp.int32, (rows, 1), 0)
    hb_ref[...] = jnp.where(pos > 0, filt[:, hw:2 * hw] * decay, 0.0)


def _hyena_filters(length, w1, b1, w2, b2, w3, b3, w4, freq):
    hid = w1.shape[1]
    hw = w4.shape[1] // 2
    bands = (HY_EMB_DIM - 1) // 2
    t01 = jnp.linspace(0.0, 1.0, length, dtype=F32)[:, None]
    w_pos = 2.0 * math.pi * jnp.arange(length, dtype=F32) / length
    f = jnp.linspace(1e-4, bands - 1, bands, dtype=F32)
    ang = w_pos[:, None] * f[None, :]
    z = jnp.concatenate([t01, jnp.cos(ang), -jnp.sin(ang),
                         jnp.zeros((length, EMB_PAD - HY_EMB_DIM), F32)], axis=-1)
    w1p = jnp.concatenate([w1, jnp.zeros((EMB_PAD - HY_EMB_DIM, hid), F32)], axis=0)
    deltas = jnp.abs(jnp.linspace(math.log(HY_DECAY_TARGET) / HY_DECAY_FAST,
                                  math.log(HY_DECAY_TARGET) / HY_DECAY_SLOW, hw, dtype=F32))[None, :]
    rows = min(length, 1024)
    out = pl.BlockSpec((rows, hw), lambda i: (i, 0))
    return pl.pallas_call(
        functools.partial(_filter_kernel, rows=rows, hw=hw),
        grid=(length // rows,),
        in_specs=[pl.BlockSpec((rows, EMB_PAD), lambda i: (i, 0)),
                  _const_spec(w1p.shape), _const_spec((1, hid)),
                  _const_spec(w2.shape), _const_spec((1, hid)),
                  _const_spec(w3.shape), _const_spec((1, hid)),
                  _const_spec(w4.shape), _const_spec(freq.shape), _const_spec(deltas.shape)],
        out_specs=[out, out],
        out_shape=[jax.ShapeDtypeStruct((length, hw), F32)] * 2,
        compiler_params=_cparams(1),
        name="hyena_filter",
    )(z, w1p, b1.reshape(1, hid), w2, b2.reshape(1, hid), w3, b3.reshape(1, hid), w4, freq, deltas)


def _dft_tables(seq):
    n2 = DFT_N2
    h1 = seq // n2
    n1 = 2 * h1
    hk = n1 // 2
    n = n1 * n2
    k1 = np.arange(hk)[:, None]
    t1 = np.arange(h1)[None, :]
    a1 = 2.0 * np.pi * ((k1 * t1) % n1) / n1
    nyquist = np.where(t1 % 2 == 0, 1.0, -1.0)
    twice = np.where(k1 == 0, 1.0, 2.0)
    f1_im, f3_im = -np.sin(a1), -twice * np.sin(a1)
    f1_im[0:1], f3_im[0:1] = nyquist, nyquist
    f1 = np.concatenate([np.cos(a1), f1_im], axis=0)
    f3 = np.concatenate([(twice * np.cos(a1)).T, f3_im.T], axis=1) / n
    kk1 = np.arange(hk + 1)[:, None, None]
    k2 = np.arange(n2)[None, :, None]
    t2 = np.arange(n2)[None, None, :]
    th = 2.0 * np.pi * ((t2 * k2 * n1 + t2 * kk1) % n) / n
    wr, wi = np.cos(th), -np.sin(th)
    m2f = np.concatenate([np.concatenate([wr, -wi], axis=2),
                          np.concatenate([wi, wr], axis=2)], axis=1)
    m2i = np.transpose(m2f, (0, 2, 1))
    as_bf16 = lambda a: jnp.asarray(a, F32).astype(BF16)
    return as_bf16(f1), as_bf16(m2f), as_bf16(m2i), as_bf16(f3)


def _pitch(rows):
    return rows + F32_SUBLANES


def _dft_stage1(load_rows, f1_ref, a_ref, n1):
    def body(t2, carry):
        p = _dot(f1_ref[...], load_rows(t2).astype(BF16))
        a_ref[pl.ds(pl.multiple_of(t2 * _pitch(n1), F32_SUBLANES), n1), :] = p
        return carry
    lax.fori_loop(0, DFT_N2, body, 0, unroll=DFT_UNROLL)


def _dft_stage2(a_ref, m2f_ref, k1, n1):
    hk = n1 // 2
    if k1 is None:
        ar = a_ref[pl.ds(hk, DFT_N2, stride=_pitch(n1)), :]
        ai = jnp.zeros_like(ar)
        k1 = hk
    else:
        ar = a_ref[pl.ds(k1, DFT_N2, stride=_pitch(n1)), :]
        ai = jnp.where(k1 == 0, 0.0, a_ref[pl.ds(hk + k1, DFT_N2, stride=_pitch(n1)), :])
    return _dot(m2f_ref[k1], jnp.concatenate([ar, ai], axis=0).astype(BF16))


def _spectrum_kernel(hf_ref, hb_ref, f1_ref, m2f_ref, kf_ref, af_ref, ab_ref, *, n1):
    n2 = DFT_N2
    hk = n1 // 2
    _dft_stage1(lambda t2: hf_ref[pl.ds(t2, hk, stride=n2), :], f1_ref, af_ref, n1)
    _dft_stage1(lambda t2: hb_ref[pl.ds(t2, hk, stride=n2), :], f1_ref, ab_ref, n1)

    def spectrum(k1):
        xf = _dft_stage2(af_ref, m2f_ref, k1, n1)
        xb = _dft_stage2(ab_ref, m2f_ref, k1, n1)
        sign = jnp.where(lax.broadcasted_iota(jnp.int32, (2 * n2, 1), 0) < n2, 1.0, -1.0)
        return xf + sign * xb

    def body(k1, carry):
        kf_ref[k1] = spectrum(k1)
        return carry
    lax.fori_loop(0, hk, body, 0, unroll=DFT_UNROLL)
    kf_ref[hk] = spectrum(None)


def _filter_spectrum(hf, hb, f1, m2f):
    s, hw = hf.shape
    n2 = DFT_N2
    n1 = 2 * s // n2
    cb = LANES
    col = pl.BlockSpec((s, cb), lambda j: (0, j))
    return pl.pallas_call(
        functools.partial(_spectrum_kernel, n1=n1),
        grid=(hw // cb,),
        in_specs=[col, col, _const_spec(f1.shape), _const_spec(m2f.shape)],
        out_specs=pl.BlockSpec((n1 // 2 + 1, 2 * n2, cb), lambda j: (0, 0, j)),
        out_shape=jax.ShapeDtypeStruct((n1 // 2 + 1, 2 * n2, hw), F32),
        scratch_shapes=[pltpu.VMEM((n2 * _pitch(n1), cb), F32), pltpu.VMEM((n2 * _pitch(n1), cb), F32)],
        compiler_params=_cparams(1),
        name="filter_spectrum",
    )(hf, hb, f1, m2f)


def _longconv_kernel(x0_ref, xv_ref, kf_ref, bias_ref, f1_ref, m2f_ref, m2i_ref, f3_ref, o_ref,
                     xs_ref, a_ref, b_ref, y_ref, *, n1):
    n2 = DFT_N2
    h1 = hk = n1 // 2

    def pitch_rows(t1, carry):
        xs_ref[pl.ds(pl.multiple_of(t1 * _pitch(n2), F32_SUBLANES), n2), :] = (
            xv_ref[0, pl.ds(pl.multiple_of(t1 * n2, n2), n2), :])
        return carry
    lax.fori_loop(0, h1, pitch_rows, 0, unroll=DFT_UNROLL)
    _dft_stage1(lambda t2: xs_ref[pl.ds(t2, h1, stride=_pitch(n2)), :], f1_ref, a_ref, n1)

    def filtered(k1, row):
        x = _dft_stage2(a_ref, m2f_ref, k1, n1)
        kf = kf_ref[row]
        xr, xi = x[0:n2], x[n2:2 * n2]
        kr, ki = kf[0:n2], kf[n2:2 * n2]
        y = jnp.concatenate([xr * kr - xi * ki, xr * ki + xi * kr], axis=0).astype(BF16)
        return _dot(m2i_ref[row], y)

    def per_k1(k1, carry):
        b_ref[pl.ds(pl.multiple_of(k1 * _pitch(2 * n2), F32_SUBLANES), 2 * n2), :] = filtered(k1, k1)
        return carry
    lax.fori_loop(0, hk, per_k1, 0, unroll=DFT_UNROLL)
    b_ref[pl.ds(n2, n2), :] = filtered(None, hk)[0:n2]

    def per_t2(t2, carry):
        br = b_ref[pl.ds(t2, hk, stride=_pitch(2 * n2)), :]
        bi = b_ref[pl.ds(n2 + t2, hk, stride=_pitch(2 * n2)), :]
        y = _dot(f3_ref[...], jnp.concatenate([br, bi], axis=0).astype(BF16))
        y_ref[pl.ds(pl.multiple_of(t2 * _pitch(h1), F32_SUBLANES), h1), :] = y
        return carry
    lax.fori_loop(0, n2, per_t2, 0, unroll=DFT_UNROLL)

    def per_t1(t1, carry):
        rows = pl.ds(pl.multiple_of(t1 * n2, n2), n2)
        y = y_ref[pl.ds(t1, n2, stride=_pitch(h1)), :]
        o_ref[0, rows, :] = x0_ref[0, rows, :] * (y + xv_ref[0, rows, :] * bias_ref[...])
        return carry
    lax.fori_loop(0, h1, per_t1, 0, unroll=DFT_UNROLL)


def _longconv(x0, xv, kf, bias, f1, m2f, m2i, f3):
    b, s, hw = x0.shape
    n2 = DFT_N2
    n1 = 2 * s // n2
    cb = LANES
    tok = pl.BlockSpec((1, s, cb), lambda j, bb: (bb, 0, j))
    return pl.pallas_call(
        functools.partial(_longconv_kernel, n1=n1),
        grid=(hw // cb, b),
        in_specs=[tok, tok,
                  pl.BlockSpec((n1 // 2 + 1, 2 * n2, cb), lambda j, bb: (0, 0, j), pipeline_mode=pl.Buffered(1)),
                  pl.BlockSpec((1, cb), lambda j, bb: (0, j)),
                  _const_spec(f1.shape), _const_spec(m2f.shape), _const_spec(m2i.shape),
                  _const_spec(f3.shape)],
        out_specs=tok,
        out_shape=jax.ShapeDtypeStruct((b, s, hw), F32),
        scratch_shapes=[pltpu.VMEM((n1 // 2 * _pitch(n2), cb), F32), pltpu.VMEM((n2 * _pitch(n1), cb), F32),
                        pltpu.VMEM((n1 // 2 * _pitch(2 * n2), cb), F32), pltpu.VMEM((n2 * _pitch(n1 // 2), cb), F32)],
        compiler_params=_cparams(2),
        name="longconv",
    )(x0, xv, kf, bias.reshape(1, hw), f1, m2f, m2i, f3)


NA_QROWS = NA_WIN_ROWS // 2
NA_BAND_ROWS = NA_QROWS + NA_WIN_ROWS


def _bias_kernel(rpb_ref, o_ref, *, n_dc):
    h = pl.program_id(0)
    n_dr = 2 * NA_WIN_ROWS - 1
    qcol = lax.broadcasted_iota(jnp.int32, (GRID_W, LANES), 0)
    lane = lax.broadcasted_iota(jnp.int32, (GRID_W, LANES), 1)
    kcol = lane % GRID_W
    win = jnp.clip(qcol - NA_WIN_COLS // 2, 0, GRID_W - NA_WIN_COLS)
    col_ok = (kcol >= win) & (kcol < win + NA_WIN_COLS)
    low_half = lane < GRID_W
    lane_row = lax.broadcasted_iota(jnp.int32, (1, LANES), 1)
    toeplitz = []
    for dr in range(n_dr):
        vec = jnp.zeros((1, LANES), F32)
        for d in range(n_dc):
            vec = jnp.where(lane_row == d, rpb_ref[(h * n_dr + dr) * n_dc + d], vec)
        rows = jnp.broadcast_to(vec * LOG2_E, (GRID_W, LANES))
        toeplitz.append([pltpu.roll(rows, (LANES - (NA_WIN_COLS - 1) + half * GRID_W) % LANES, 1,
                                    stride=1, stride_axis=0) for half in range(2)])
    neg = jnp.full((GRID_W, LANES), NEG_BIG, F32)
    for pos in range(3):
        for p in range(NA_QROWS):
            lo = (0, p, NA_QROWS)[pos]
            for jj in range(NA_BAND_ROWS // 2):
                halves = []
                for half in range(2):
                    j = 2 * jj + half
                    dr = j - p + NA_WIN_ROWS - 1 - NA_QROWS * pos
                    halves.append(toeplitz[dr][half] if lo <= j < lo + NA_WIN_ROWS else neg)
                tile = jnp.where(low_half, halves[0], halves[1])
                o_ref[pos, 0, p * GRID_W:(p + 1) * GRID_W, jj * LANES:(jj + 1) * LANES] = (
                    jnp.where(col_ok, tile, NEG_BIG))


def _bias_table(rpb):
    heads, n_dr, n_dc = rpb.shape
    blk = (NA_QROWS * GRID_W, NA_BAND_ROWS * GRID_W)
    assert 2 * GRID_W == LANES and NA_BAND_ROWS % 2 == 0 and n_dc <= LANES
    return pl.pallas_call(
        functools.partial(_bias_kernel, n_dc=n_dc),
        grid=(heads,),
        in_specs=[pl.BlockSpec(memory_space=pltpu.SMEM)],
        out_specs=pl.BlockSpec((3, 1) + blk, lambda h: (0, h, 0, 0)),
        out_shape=jax.ShapeDtypeStruct((3, heads) + blk, F32),
        compiler_params=_cparams(1),
        name="na_bias",
    )(rpb.reshape(-1))


def _nattn_kernel(q_ref, k_ref, v_ref, kc_ref, vc_ref, bias_ref, o_ref, vs_ref, vcs_ref, *, rows):
    dh = NA_HEAD_DIM
    nq = NA_QROWS * GRID_W
    band = NA_BAND_ROWS * GRID_W
    nt = (((1,), (1,)), ((), ()))

    def own_lanes(n_rows):
        lane = lax.broadcasted_iota(jnp.int32, (n_rows, LANES), 1)
        return [lane < dh, lane >= dh]

    for hh in range(2):
        vs_ref[hh] = jnp.where(own_lanes(v_ref.shape[1])[hh], v_ref[0], 1.0).astype(BF16)
        vcs_ref[hh] = jnp.where(own_lanes(vc_ref.shape[1])[hh], vc_ref[0], 1.0).astype(BF16)
    own_q = own_lanes(nq)
    n_blocks = rows // NA_QROWS

    def per_block(i, carry):
        r0 = i * NA_QROWS
        pos = jnp.where(i == 0, 0, jnp.where(i == n_blocks - 1, 2, 1))
        key0 = pl.multiple_of(jnp.clip(r0 - NA_WIN_ROWS // 2, 0, rows - NA_BAND_ROWS) * GRID_W, GRID_W)
        qrows = pl.ds(pl.multiple_of(r0 * GRID_W, nq), nq)
        q = q_ref[0, qrows, :]
        kb = k_ref[0, pl.ds(key0, band), :]
        outs = []
        for hh in range(2):
            qh = jnp.where(own_q[hh], q, 0.0).astype(BF16)
            s_loc = lax.dot_general(qh, kb, nt, preferred_element_type=F32) + bias_ref[pos, hh]
            s_ctx = lax.dot_general(qh, kc_ref[0], nt, preferred_element_type=F32)
            s = jnp.concatenate([s_loc, s_ctx], axis=-1)
            p = jnp.exp2(s - jnp.max(s, axis=-1, keepdims=True)).astype(BF16)
            o = _dot(p[:, 0:band], vs_ref[hh, pl.ds(key0, band), :]) + _dot(p[:, band:], vcs_ref[hh])
            outs.append(o / pltpu.roll(o, dh, axis=1))
        o_ref[0, qrows, :] = jnp.where(own_q[0], outs[0], outs[1]).astype(BF16)
        return carry
    lax.fori_loop(0, n_blocks, per_block, 0, unroll=NA_UNROLL)


def _nattn(q, k, v, kc, vc, bias):
    b, s, aw = q.shape
    nctx = kc.shape[1]
    hps = LANES // NA_HEAD_DIM
    assert (s // GRID_W) % NA_QROWS == 0 and s // GRID_W >= NA_BAND_ROWS
    tok = pl.BlockSpec((1, s, LANES), lambda j, bb: (bb, 0, j))
    ctx = pl.BlockSpec((1, nctx, LANES), lambda j, bb: (bb, 0, j))
    return pl.pallas_call(
        functools.partial(_nattn_kernel, rows=s // GRID_W),
        grid=(aw // LANES, b),
        in_specs=[tok, tok, tok, ctx, ctx,
                  pl.BlockSpec((3, hps) + bias.shape[2:], lambda j, bb: (0, j, 0, 0))],
        out_specs=tok,
        out_shape=jax.ShapeDtypeStruct((b, s, aw), BF16),
        scratch_shapes=[pltpu.VMEM((hps, s, LANES), BF16), pltpu.VMEM((hps, nctx, LANES), BF16)],
        compiler_params=_cparams(2),
        name="nattn",
    )(q, k, v, kc, vc, bias)


def _outproj_kernel(yh_ref, ya_ref, h_ref, mod_ref, g_ref, w_ref, o_ref, *, d, hw):
    gate = mod_ref[0, :, 2 * d:3 * d]
    y = _dot(yh_ref[0].astype(BF16), w_ref[0:hw, :]) + _dot(ya_ref[0], w_ref[hw:, :])
    o_ref[0] = h_ref[0] + gate * _rms(y, g_ref[...])


def _outproj(y_hy, y_na, h, mod, g, w_out, tile):
    b, s, d = h.shape
    hw = y_hy.shape[2]
    aw = y_na.shape[2]
    tok = lambda width: pl.BlockSpec((1, tile, width), lambda bb, i: (bb, i, 0))
    return pl.pallas_call(
        functools.partial(_outproj_kernel, d=d, hw=hw),
        grid=(b, s // tile),
        in_specs=[tok(hw), tok(aw), tok(d),
                  pl.BlockSpec((1, 1, mod.shape[2]), lambda bb, i: (bb, 0, 0)),
                  _const_spec((1, d)), _const_spec(w_out.shape)],
        out_specs=tok(d),
        out_shape=jax.ShapeDtypeStruct((b, s, d), F32),
        compiler_params=_cparams(2),
        name="outproj",
    )(y_hy, y_na, h, mod, g.reshape(1, d), w_out)


def _ffn_kernel(prev_ref, main_ref, next_ref, mod_ref, gpre_ref, gpost_ref, wup_ref, cw_ref, cb_ref, wdn_ref,
                o_ref, hn_ref, z_ref, acc_ref, *, tile, d, n_chunks):
    fc = FFN_CHUNK
    shift = mod_ref[0, :, 3 * d:4 * d]
    scale = mod_ref[0, :, 4 * d:5 * d]
    gate = mod_ref[0, :, 5 * d:6 * d]
    g = gpre_ref[...]
    _fill_halo_tile(hn_ref, prev_ref, main_ref, next_ref,
                    lambda t: _mod_norm(t, g, scale, shift), tile)
    acc_ref[...] = jnp.zeros_like(acc_ref)

    def up(j, slot):
        z_ref[slot] = _dot(hn_ref[...], wup_ref[j])

    def down(j, slot):
        cw = cw_ref[j]
        c = (cw[0:1, :] * z_ref[slot, pl.ds(HALO - 1, tile), :]
             + cw[1:2, :] * z_ref[slot, pl.ds(HALO, tile), :]
             + cw[2:3, :] * z_ref[slot, pl.ds(HALO + 1, tile), :]
             + cb_ref[j])
        a = (_silu(c[:, 0:fc]) * c[:, fc:2 * fc]).astype(BF16)
        acc_ref[...] += _dot(a, wdn_ref[j])

    def step(j, slot):
        up(j + 1, 1 - slot)
        down(j, slot)

    up(0, 0)
    n_pairs = (n_chunks - 1) // 2

    def pair(jj, carry):
        step(2 * jj, 0)
        step(2 * jj + 1, 1)
        return carry
    lax.fori_loop(0, n_pairs, pair, 0)
    if (n_chunks - 1) % 2:
        step(n_chunks - 2, 0)
    down(n_chunks - 1, (n_chunks - 1) % 2)
    o_ref[0] = main_ref[0] + gate * _rms(acc_ref[...], gpost_ref[...])


def _ffn(h, mod, g_pre, g_post, w_up, w_dw, b_dw, w_down, tile):
    b, s, d = h.shape
    dff = w_down.shape[0]
    fc = FFN_CHUNK
    nch = dff // fc
    pair = lambda t: jnp.concatenate([t[..., :dff].reshape(t.shape[:-1] + (nch, fc)),
                                      t[..., dff:].reshape(t.shape[:-1] + (nch, fc))], axis=-1)
    wup = jnp.transpose(pair(w_up), (1, 0, 2)).astype(BF16)
    cw = jnp.transpose(pair(w_dw), (1, 0, 2))
    cb = pair(b_dw).reshape(nch, 1, 2 * fc)
    wdn = w_down.reshape(nch, fc, d).astype(BF16)
    return pl.pallas_call(
        functools.partial(_ffn_kernel, tile=tile, d=d, n_chunks=nch),
        grid=(b, s // tile),
        in_specs=_halo_specs(tile, d, s) + [
            pl.BlockSpec((1, 1, mod.shape[2]), lambda bb, i: (bb, 0, 0)),
            _const_spec((1, d)), _const_spec((1, d)),
            _const_spec(wup.shape), _const_spec(cw.shape), _const_spec(cb.shape), _const_spec(wdn.shape)],
        out_specs=pl.BlockSpec((1, tile, d), lambda bb, i: (bb, i, 0)),
        out_shape=jax.ShapeDtypeStruct((b, s, d), F32),
        scratch_shapes=[pltpu.VMEM((tile + 2 * HALO, d), BF16),
                        pltpu.VMEM((2, tile + 2 * HALO, 2 * fc), F32),
                        pltpu.VMEM((tile, d), F32)],
        compiler_params=_cparams(2),
        name="convffn",
    )(h, h, h, mod, g_pre.reshape(1, d), g_post.reshape(1, d), wup, cw, cb, wdn)


CONF_ROWS = 64
CONF_COLS = LANES


def _conformer_kernel(prev_ref, main_ref, next_ref, mod_ref, gpre_ref, gpost_ref, w1_ref, b1_ref,
                      dw_ref, db_ref, lg_ref, lb_ref, w2_ref, b2_ref, o_ref, hn_ref, u_ref, c_ref,
                      *, tile, d, seq):
    shift = mod_ref[0, :, 0:d]
    scale = mod_ref[0, :, d:2 * d]
    gate = mod_ref[0, :, 2 * d:3 * d]
    g = gpre_ref[...]
    _fill_halo_tile(hn_ref, prev_ref, main_ref, next_ref,
                    lambda t: _mod_norm(t, g, scale, shift), tile)
    ag = _dot(hn_ref[...], w1_ref[...]) + b1_ref[...]
    pos = (pl.program_id(1) * tile - HALO
           + lax.broadcasted_iota(jnp.int32, (tile + 2 * HALO, 1), 0))
    u_ref[...] = jnp.where((pos >= 0) & (pos < seq), ag[:, 0:d] * jax.nn.sigmoid(ag[:, d:2 * d]), 0.0)
    first = HALO - CONF_K // 2
    reach = -(-(first + CONF_K - 1) // F32_SUBLANES)

    def conv_block(rb, carry):
        r0 = pl.multiple_of(rb * CONF_ROWS, CONF_ROWS)
        for c0 in range(0, d, CONF_COLS):
            cols = slice(c0, c0 + CONF_COLS)
            slab = u_ref[pl.ds(r0, CONF_ROWS + 2 * HALO), cols]
            acc = jnp.broadcast_to(db_ref[:, cols], (CONF_ROWS, CONF_COLS))
            n_slab = CONF_ROWS + 2 * HALO
            for s in range(F32_SUBLANES):
                shifted = pltpu.roll(slab, (n_slab - s) % n_slab, 0) if s else slab
                for a in range(reach):
                    k = a * F32_SUBLANES + s - first
                    if 0 <= k < CONF_K:
                        acc = acc + dw_ref[k:k + 1, cols] * shifted[a * F32_SUBLANES:a * F32_SUBLANES + CONF_ROWS]
            c_ref[pl.ds(r0, CONF_ROWS), cols] = acc
        return carry
    lax.fori_loop(0, tile // CONF_ROWS, conv_block, 0)
    c = c_ref[...]
    mu = jnp.mean(c, axis=-1, keepdims=True)
    cc = c - mu
    var = jnp.mean(cc * cc, axis=-1, keepdims=True)
    ln = cc * lax.rsqrt(var + EPS) * lg_ref[...] + lb_ref[...]
    y = _dot(_silu(ln).astype(BF16), w2_ref[...]) + b2_ref[...]
    o_ref[0] = main_ref[0] + gate * _rms(y, gpost_ref[...])


def _conformer(h, mod, g_pre, g_post, w_pw1, b_pw1, w_dw, b_dw, ln_g, ln_b, w_pw2, b_pw2, tile):
    b, s, d = h.shape
    row = lambda t: t.reshape(1, -1)
    return pl.pallas_call(
        functools.partial(_conformer_kernel, tile=tile, d=d, seq=s),
        grid=(b, s // tile),
        in_specs=_halo_specs(tile, d, s) + [
            pl.BlockSpec((1, 1, mod.shape[2]), lambda bb, i: (bb, 0, 0)),
            _const_spec((1, d)), _const_spec((1, d)),
            _const_spec(w_pw1.shape), _const_spec((1, 2 * d)),
            _const_spec(w_dw.shape), _const_spec((1, d)), _const_spec((1, d)), _const_spec((1, d)),
            _const_spec(w_pw2.shape), _const_spec((1, d))],
        out_specs=pl.BlockSpec((1, tile, d), lambda bb, i: (bb, i, 0)),
        out_shape=jax.ShapeDtypeStruct((b, s, d), F32),
        scratch_shapes=[pltpu.VMEM((tile + 2 * HALO, d), BF16),
                        pltpu.VMEM((tile + 2 * HALO, d), F32),
                        pltpu.VMEM((tile, d), F32)],
        compiler_params=_cparams(2),
        name="conformer",
    )(h, h, h, mod, row(g_pre), row(g_post), w_pw1.astype(BF16), row(b_pw1), w_dw, row(b_dw),
      row(ln_g), row(ln_b), w_pw2.astype(BF16), row(b_pw2))


def _token_tile(seq):
    return min(seq, 1024)


def kernel(x, c, ctx, c_ctx, w_mod, b_mod, g_mix_pre, g_mix_post, g_ffn_pre, g_ffn_post, w_in, w_out, hy_short_w, hy_short_b, hy_f_w1, hy_f_b1, hy_f_w2, hy_f_b2, hy_f_w3, hy_f_b3, hy_f_w4, hy_f_freq, hy_bias, na_rpb, cf_w_pw1, cf_b_pw1, cf_w_dw, cf_b_dw, cf_ln_g, cf_ln_b, cf_w_pw2, cf_b_pw2, ffn_w_up, ffn_w_dw, ffn_b_dw, ffn_w_down):
    bsz, seq, d = x.shape
    depth = w_mod.shape[0]
    hw = hy_bias.shape[1]
    aw = NA_HEADS * NA_HEAD_DIM
    tile = _token_tile(seq)
    assert seq % tile == 0 and seq % (GRID_W * NA_WIN_ROWS) == 0 and tile % HALO == 0

    n_rows = -(-(bsz + 1) // BF16_SUBLANES) * BF16_SUBLANES
    cc = jnp.concatenate([c, c_ctx[None, :], jnp.zeros((n_rows - bsz - 1, d), F32)], axis=0)
    mod_all = _adaln(cc, w_mod, b_mod)

    h = x
    for layer in range(depth):
        mod = mod_all[layer, :bsz].reshape(bsz, 1, 6 * d)
        if layer % 2 == 0:
            e = layer // 2
            w_e = w_in[e].astype(BF16)
            k_ctx, v_ctx = _ctxkv(ctx, mod_all[layer, bsz:bsz + 1, :2 * d], g_mix_pre[layer],
                                  w_e[:, 3 * hw + aw:])
            x0, xv, q, k, v = _inproj(h, mod, g_mix_pre[layer], w_e[:, :3 * hw], w_e[:, 3 * hw:],
                                      hy_short_w[e], hy_short_b[e], tile)
            f1, m2f, m2i, f3 = _dft_tables(seq)
            hf, hb = _hyena_filters(seq, hy_f_w1[e], hy_f_b1[e], hy_f_w2[e], hy_f_b2[e],
                                    hy_f_w3[e], hy_f_b3[e], hy_f_w4[e], hy_f_freq[e])
            kf = _filter_spectrum(hf, hb, f1, m2f)
            y_hy = _longconv(x0, xv, kf, hy_bias[e], f1, m2f, m2i, f3)
            y_na = _nattn(q, k, v, k_ctx, v_ctx, _bias_table(na_rpb[e]))
            h = _outproj(y_hy, y_na, h, mod, g_mix_post[layer], w_out[e].astype(BF16), tile)
        else:
            o = layer // 2
            h = _conformer(h, mod, g_mix_pre[layer], g_mix_post[layer], cf_w_pw1[o], cf_b_pw1[o],
                           cf_w_dw[o], cf_b_dw[o], cf_ln_g[o], cf_ln_b[o], cf_w_pw2[o], cf_b_pw2[o], tile)
        h = _ffn(h, mod, g_ffn_pre[layer], g_ffn_post[layer], ffn_w_up[layer], ffn_w_dw[layer],
                 ffn_b_dw[layer], ffn_w_down[layer], tile)
    return h
```

```python
import functools
import math

import numpy as np
import jax
import jax.numpy as jnp
from jax import lax
from jax.experimental import pallas as pl
from jax.experimental.pallas import tpu as pltpu

F32 = jnp.float32
BF16 = jnp.bfloat16

EPS = 1e-6
NEG_BIG = -1e30
LOG2_E = math.log2(math.e)

GRID_W = 64
NA_HEADS = 8
NA_HEAD_DIM = 64
NA_WIN_ROWS = 8
NA_WIN_COLS = 16
HY_SHORT_K = 3
HY_EMB_DIM = 33
HY_DECAY_FAST = 0.3
HY_DECAY_SLOW = 1.5
HY_DECAY_TARGET = 1e-2
CONF_K = 31
FFN_CONV_K = 3

LANES = 128
F32_SUBLANES = 8
BF16_SUBLANES = 16
VMEM_LIMIT = 56 * 1024 * 1024

HALO = BF16_SUBLANES
DFT_N2 = 64
DFT_UNROLL = 16
NA_UNROLL = 4
FFN_CHUNK = 256
EMB_PAD = LANES


def _cparams(n_axes):
    return pltpu.CompilerParams(dimension_semantics=("arbitrary",) * n_axes,
                                vmem_limit_bytes=VMEM_LIMIT)


def _const_spec(shape):
    nd = len(shape)
    return pl.BlockSpec(shape, lambda *_: (0,) * nd, pipeline_mode=pl.Buffered(1))


def _silu(t):
    return t * jax.nn.sigmoid(t)


def _mod_norm(t, g, scale, shift):
    ms = jnp.mean(t * t, axis=-1, keepdims=True)
    return (t * lax.rsqrt(ms + EPS) * g) * (1.0 + scale) + shift


def _rms(t, g):
    ms = jnp.mean(t * t, axis=-1, keepdims=True)
    return t * lax.rsqrt(ms + EPS) * g


def _dot(a, b):
    return jnp.dot(a, b, preferred_element_type=F32)


def _dot_hi(a, b):
    return jnp.dot(a, b, preferred_element_type=F32, precision=lax.Precision.HIGHEST)


def _fill_halo_tile(hn_ref, prev_ref, main_ref, next_ref, norm_fn, tile):
    i = pl.program_id(1)
    last = pl.num_programs(1) - 1
    p = norm_fn(prev_ref[0])
    n = norm_fn(next_ref[0])
    hn_ref[0:HALO, :] = jnp.where(i > 0, p, 0.0).astype(BF16)
    hn_ref[HALO:HALO + tile, :] = norm_fn(main_ref[0]).astype(BF16)
    hn_ref[HALO + tile:, :] = jnp.where(i < last, n, 0.0).astype(BF16)


def _halo_specs(tile, d, seq):
    r = tile // HALO
    nblk = seq // HALO
    return [
        pl.BlockSpec((1, HALO, d), lambda b, i: (b, jnp.maximum(i * r - 1, 0), 0)),
        pl.BlockSpec((1, tile, d), lambda b, i: (b, i, 0)),
        pl.BlockSpec((1, HALO, d), lambda b, i: (b, jnp.minimum((i + 1) * r, nblk - 1), 0)),
    ]


def _adaln_kernel(c_ref, w_ref, b_ref, o_ref):
    s = _silu(c_ref[...]).astype(BF16)
    o_ref[0] = _dot(s, w_ref[0].astype(BF16)) + b_ref[0]


def _adaln(cc, w_mod, b_mod):
    depth, d, n6 = w_mod.shape
    r = cc.shape[0]
    nb = d
    return pl.pallas_call(
        _adaln_kernel,
        grid=(depth, n6 // nb),
        in_specs=[pl.BlockSpec((r, d), lambda l, j: (0, 0)),
                  pl.BlockSpec((1, d, nb), lambda l, j: (l, 0, j)),
                  pl.BlockSpec((1, 1, nb), lambda l, j: (l, 0, j))],
        out_specs=pl.BlockSpec((1, r, nb), lambda l, j: (l, 0, j)),
        out_shape=jax.ShapeDtypeStruct((depth, r, n6), F32),
        compiler_params=_cparams(2),
        name="adaln",
    )(cc, w_mod, b_mod.reshape(depth, 1, n6))


def _inproj_kernel(prev_ref, main_ref, next_ref, mod_ref, g_ref, why_ref, wqkv_ref, sw_ref, sb_ref,
                   x0_ref, xv_ref, q_ref, k_ref, v_ref, hn_ref, z_ref, *, tile, d, hw, aw):
    shift = mod_ref[0, :, 0:d]
    scale = mod_ref[0, :, d:2 * d]
    g = g_ref[...]
    _fill_halo_tile(hn_ref, prev_ref, main_ref, next_ref,
                    lambda t: _mod_norm(t, g, scale, shift), tile)
    z_ref[...] = _dot(hn_ref[...], why_ref[...])
    uc = (sw_ref[0:1, :] * z_ref[pl.ds(HALO - 1, tile), :]
          + sw_ref[1:2, :] * z_ref[pl.ds(HALO, tile), :]
          + sw_ref[2:3, :] * z_ref[pl.ds(HALO + 1, tile), :]
          + sb_ref[...])
    x0_ref[0] = uc[:, 0:hw]
    xv_ref[0] = uc[:, hw:2 * hw] * uc[:, 2 * hw:3 * hw]
    qkv = _dot(hn_ref[pl.ds(HALO, tile), :], wqkv_ref[...])
    q_ref[0] = (qkv[:, 0:aw] * (NA_HEAD_DIM ** -0.5 * LOG2_E)).astype(BF16)
    k_ref[0] = qkv[:, aw:2 * aw].astype(BF16)
    v_ref[0] = qkv[:, 2 * aw:3 * aw].astype(BF16)


def _inproj(h, mod, g, w_hy, w_qkv, short_w, short_b, tile):
    b, s, d = h.shape
    hw = w_hy.shape[1] // 3
    aw = w_qkv.shape[1] // 3
    tok = lambda width: pl.BlockSpec((1, tile, width), lambda bb, i: (bb, i, 0))
    return pl.pallas_call(
        functools.partial(_inproj_kernel, tile=tile, d=d, hw=hw, aw=aw),
        grid=(b, s // tile),
        in_specs=_halo_specs(tile, d, s) + [
            pl.BlockSpec((1, 1, mod.shape[2]), lambda bb, i: (bb, 0, 0)),
            _const_spec((1, d)),
            _const_spec(w_hy.shape),
            _const_spec(w_qkv.shape),
            _const_spec(short_w.shape),
            _const_spec((1, 3 * hw)),
        ],
        out_specs=[tok(hw), tok(hw), tok(aw), tok(aw), tok(aw)],
        out_shape=[jax.ShapeDtypeStruct((b, s, hw), F32), jax.ShapeDtypeStruct((b, s, hw), F32),
                   jax.ShapeDtypeStruct((b, s, aw), BF16), jax.ShapeDtypeStruct((b, s, aw), BF16),
                   jax.ShapeDtypeStruct((b, s, aw), BF16)],
        scratch_shapes=[pltpu.VMEM((tile + 2 * HALO, d), BF16),
                        pltpu.VMEM((tile + 2 * HALO, 3 * hw), F32)],
        compiler_params=_cparams(2),
        name="inproj",
    )(h, h, h, mod, g.reshape(1, d), w_hy, w_qkv, short_w, short_b.reshape(1, 3 * hw))


def _ctxkv_kernel(ctx_ref, mod_ref, g_ref, w_ref, k_ref, v_ref, *, d, aw):
    shift = mod_ref[:, 0:d]
    scale = mod_ref[:, d:2 * d]
    cn = _mod_norm(ctx_ref[0], g_ref[...], scale, shift).astype(BF16)
    kv = _dot(cn, w_ref[...])
    k_ref[0] = kv[:, 0:aw].astype(BF16)
    v_ref[0] = kv[:, aw:2 * aw].astype(BF16)


def _ctxkv(ctx, mod_ctx, g, w_kv):
    b, n, d = ctx.shape
    aw = w_kv.shape[1] // 2
    out = pl.BlockSpec((1, n, aw), lambda bb: (bb, 0, 0))
    return pl.pallas_call(
        functools.partial(_ctxkv_kernel, d=d, aw=aw),
        grid=(b,),
        in_specs=[pl.BlockSpec((1, n, d), lambda bb: (bb, 0, 0)),
                  _const_spec(mod_ctx.shape), _const_spec((1, d)), _const_spec(w_kv.shape)],
        out_specs=[out, out],
        out_shape=[jax.ShapeDtypeStruct((b, n, aw), BF16)] * 2,
        compiler_params=_cparams(1),
        name="ctxkv",
    )(ctx, mod_ctx, g.reshape(1, d), w_kv)


def _filter_kernel(z_ref, w1_ref, b1_ref, w2_ref, b2_ref, w3_ref, b3_ref, w4_ref, fr_ref, dl_ref,
                   hf_ref, hb_ref, *, rows, hw):
    half = rows // 2
    z = jnp.concatenate([z_ref[0:half, :], z_ref[half:rows, :]], axis=1)
    hdn = jnp.sin(fr_ref[0:1, :] * (_dot_hi(z, w1_ref[...]) + b1_ref[...]))
    hdn = jnp.sin(fr_ref[1:2, :] * (_dot_hi(hdn, w2_ref[...]) + b2_ref[...]))
    hdn = jnp.sin(fr_ref[2:3, :] * (_dot_hi(hdn, w3_ref[...]) + b3_ref[...]))
    filt = _dot_hi(hdn, w4_ref[...])
    for part in range(2):
        rs = slice(part * half, (part + 1) * half)
        f = filt[:, part * 2 * hw:(part + 1) * 2 * hw]
        decay = jnp.exp(-z_ref[rs, 0:1] * dl_ref[...])
        hf_ref[rs, :] = f[:, 0:hw] * decay
        pos = pl.program_id(0) * rows + part * half + lax.broadcasted_iota(jnp.int32, (half, 1), 0)
        hb_ref[rs, :] = jnp.where(pos > 0, f[:, hw:2 * hw] * decay, 0.0)


def _hyena_filters(length, w1, b1, w2, b2, w3, b3, w4, freq):
    hid = w1.shape[1]
    hw = w4.shape[1] // 2
    bands = (HY_EMB_DIM - 1) // 2
    t01 = jnp.linspace(0.0, 1.0, length, dtype=F32)[:, None]
    w_pos = 2.0 * math.pi * jnp.arange(length, dtype=F32) / length
    f = jnp.linspace(1e-4, bands - 1, bands, dtype=F32)
    ang = w_pos[:, None] * f[None, :]
    z = jnp.concatenate([t01, jnp.cos(ang), -jnp.sin(ang),
                         jnp.zeros((length, EMB_PAD - HY_EMB_DIM), F32)], axis=-1)
    w1p = jnp.concatenate([w1, jnp.zeros((EMB_PAD - HY_EMB_DIM, hid), F32)], axis=0)
    deltas = jnp.abs(jnp.linspace(math.log(HY_DECAY_TARGET) / HY_DECAY_FAST,
                                  math.log(HY_DECAY_TARGET) / HY_DECAY_SLOW, hw, dtype=F32))[None, :]
    rows = min(length, 1024)
    out = pl.BlockSpec((rows, hw), lambda i: (i, 0))
    twice = lambda v: jnp.concatenate([v, v], axis=-1)
    diag2 = lambda w: jnp.concatenate([jnp.concatenate([w, jnp.zeros_like(w)], axis=1),
                                       jnp.concatenate([jnp.zeros_like(w), w], axis=1)], axis=0)
    args = (z, diag2(w1p), twice(b1.reshape(1, hid)), diag2(w2), twice(b2.reshape(1, hid)),
            diag2(w3), twice(b3.reshape(1, hid)), diag2(w4), twice(freq), deltas)
    return pl.pallas_call(
        functools.partial(_filter_kernel, rows=rows, hw=hw),
        grid=(length // rows,),
        in_specs=[pl.BlockSpec((rows, EMB_PAD), lambda i: (i, 0))] + [_const_spec(a.shape) for a in args[1:]],
        out_specs=[out, out],
        out_shape=[jax.ShapeDtypeStruct((length, hw), F32)] * 2,
        compiler_params=_cparams(1),
        name="hyena_filter",
    )(*args)


def _dft_tables(seq):
    n2 = DFT_N2
    h1 = seq // n2
    n1 = 2 * h1
    hk = n1 // 2
    n = n1 * n2
    k1 = np.arange(hk)[:, None]
    t1 = np.arange(h1)[None, :]
    a1 = 2.0 * np.pi * ((k1 * t1) % n1) / n1
    nyquist = np.where(t1 % 2 == 0, 1.0, -1.0)
    twice = np.where(k1 == 0, 1.0, 2.0)
    f1_im, f3_im = -np.sin(a1), -twice * np.sin(a1)
    f1_im[0:1], f3_im[0:1] = nyquist, nyquist
    f1 = np.concatenate([np.cos(a1), f1_im], axis=0)
    f3 = np.concatenate([(twice * np.cos(a1)).T, f3_im.T], axis=1) / n
    kk1 = np.arange(hk + 1)[:, None, None]
    k2 = np.arange(n2)[None, :, None]
    t2 = np.arange(n2)[None, None, :]
    th = 2.0 * np.pi * ((t2 * k2 * n1 + t2 * kk1) % n) / n
    wr, wi = np.cos(th), -np.sin(th)
    m2f = np.concatenate([np.concatenate([wr, -wi], axis=2),
                          np.concatenate([wi, wr], axis=2)], axis=1)
    m2i = np.transpose(m2f, (0, 2, 1))
    as_bf16 = lambda a: jnp.asarray(a, F32).astype(BF16)
    return as_bf16(f1), as_bf16(m2f), as_bf16(m2i), as_bf16(f3)


def _pitch(rows):
    return rows + F32_SUBLANES


def _dft_stage1(load_rows, f1_ref, a_ref, n1):
    def body(t2, carry):
        p = _dot(f1_ref[...], load_rows(t2).astype(BF16))
        a_ref[pl.ds(pl.multiple_of(t2 * _pitch(n1), F32_SUBLANES), n1), :] = p
        return carry
    lax.fori_loop(0, DFT_N2, body, 0, unroll=DFT_UNROLL)


def _dft_stage2(a_ref, m2f_ref, k1, n1):
    hk = n1 // 2
    if k1 is None:
        ar = a_ref[pl.ds(hk, DFT_N2, stride=_pitch(n1)), :]
        ai = jnp.zeros_like(ar)
        k1 = hk
    else:
        ar = a_ref[pl.ds(k1, DFT_N2, stride=_pitch(n1)), :]
        ai = jnp.where(k1 == 0, 0.0, a_ref[pl.ds(hk + k1, DFT_N2, stride=_pitch(n1)), :])
    return _dot(m2f_ref[k1], jnp.concatenate([ar, ai], axis=0).astype(BF16))


def _spectrum_kernel(hf_ref, hb_ref, f1_ref, m2f_ref, kf_ref, af_ref, ab_ref, *, n1):
    n2 = DFT_N2
    hk = n1 // 2
    _dft_stage1(lambda t2: hf_ref[pl.ds(t2, hk, stride=n2), :], f1_ref, af_ref, n1)
    _dft_stage1(lambda t2: hb_ref[pl.ds(t2, hk, stride=n2), :], f1_ref, ab_ref, n1)

    def spectrum(k1):
        xf = _dft_stage2(af_ref, m2f_ref, k1, n1)
        xb = _dft_stage2(ab_ref, m2f_ref, k1, n1)
        sign = jnp.where(lax.broadcasted_iota(jnp.int32, (2 * n2, 1), 0) < n2, 1.0, -1.0)
        return xf + sign * xb

    def body(k1, carry):
        kf_ref[k1] = spectrum(k1)
        return carry
    lax.fori_loop(0, hk, body, 0, unroll=DFT_UNROLL)
    kf_ref[hk] = spectrum(None)


def _filter_spectrum(hf, hb, f1, m2f):
    s, hw = hf.shape
    n2 = DFT_N2
    n1 = 2 * s // n2
    cb = LANES
    col = pl.BlockSpec((s, cb), lambda j: (0, j))
    return pl.pallas_call(
        functools.partial(_spectrum_kernel, n1=n1),
        grid=(hw // cb,),
        in_specs=[col, col, _const_spec(f1.shape), _const_spec(m2f.shape)],
        out_specs=pl.BlockSpec((n1 // 2 + 1, 2 * n2, cb), lambda j: (0, 0, j)),
        out_shape=jax.ShapeDtypeStruct((n1 // 2 + 1, 2 * n2, hw), F32),
        scratch_shapes=[pltpu.VMEM((n2 * _pitch(n1), cb), F32), pltpu.VMEM((n2 * _pitch(n1), cb), F32)],
        compiler_params=_cparams(1),
        name="filter_spectrum",
    )(hf, hb, f1, m2f)


def _longconv_kernel(x0_ref, xv_ref, kf_ref, bias_ref, f1_ref, m2f_ref, m2i_ref, f3_ref, o_ref,
                     xs_ref, a_ref, b_ref, y_ref, *, n1):
    n2 = DFT_N2
    h1 = hk = n1 // 2

    def pitch_rows(t1, carry):
        xs_ref[pl.ds(pl.multiple_of(t1 * _pitch(n2), F32_SUBLANES), n2), :] = (
            xv_ref[0, pl.ds(pl.multiple_of(t1 * n2, n2), n2), :])
        return carry
    lax.fori_loop(0, h1, pitch_rows, 0, unroll=DFT_UNROLL)
    _dft_stage1(lambda t2: xs_ref[pl.ds(t2, h1, stride=_pitch(n2)), :], f1_ref, a_ref, n1)

    def filtered(k1, row):
        x = _dft_stage2(a_ref, m2f_ref, k1, n1)
        kf = kf_ref[row]
        xr, xi = x[0:n2], x[n2:2 * n2]
        kr, ki = kf[0:n2], kf[n2:2 * n2]
        y = jnp.concatenate([xr * kr - xi * ki, xr * ki + xi * kr], axis=0).astype(BF16)
        return _dot(m2i_ref[row], y)

    def per_k1(k1, carry):
        b_ref[pl.ds(pl.multiple_of(k1 * _pitch(2 * n2), F32_SUBLANES), 2 * n2), :] = filtered(k1, k1)
        return carry
    lax.fori_loop(0, hk, per_k1, 0, unroll=DFT_UNROLL)
    b_ref[pl.ds(n2, n2), :] = filtered(None, hk)[0:n2]

    def per_t2(t2, carry):
        br = b_ref[pl.ds(t2, hk, stride=_pitch(2 * n2)), :]
        bi = b_ref[pl.ds(n2 + t2, hk, stride=_pitch(2 * n2)), :]
        y = _dot(f3_ref[...], jnp.concatenate([br, bi], axis=0).astype(BF16))
        y_ref[pl.ds(pl.multiple_of(t2 * _pitch(h1), F32_SUBLANES), h1), :] = y
        return carry
    lax.fori_loop(0, n2, per_t2, 0, unroll=DFT_UNROLL)

    def per_t1(t1, carry):
        rows = pl.ds(pl.multiple_of(t1 * n2, n2), n2)
        y = y_ref[pl.ds(t1, n2, stride=_pitch(h1)), :]
        o_ref[0, rows, :] = (x0_ref[0, rows, :] * (y + xv_ref[0, rows, :] * bias_ref[...])).astype(BF16)
        return carry
    lax.fori_loop(0, h1, per_t1, 0, unroll=DFT_UNROLL)


def _longconv(x0, xv, kf, bias, f1, m2f, m2i, f3):
    b, s, hw = x0.shape
    n2 = DFT_N2
    n1 = 2 * s // n2
    cb = LANES
    tok = pl.BlockSpec((1, s, cb), lambda j, bb: (bb, 0, j))
    return pl.pallas_call(
        functools.partial(_longconv_kernel, n1=n1),
        grid=(hw // cb, b),
        in_specs=[tok, tok,
                  pl.BlockSpec((n1 // 2 + 1, 2 * n2, cb), lambda j, bb: (0, 0, j), pipeline_mode=pl.Buffered(1)),
                  pl.BlockSpec((1, cb), lambda j, bb: (0, j)),
                  _const_spec(f1.shape), _const_spec(m2f.shape), _const_spec(m2i.shape),
                  _const_spec(f3.shape)],
        out_specs=tok,
        out_shape=jax.ShapeDtypeStruct((b, s, hw), BF16),
        scratch_shapes=[pltpu.VMEM((n1 // 2 * _pitch(n2), cb), F32), pltpu.VMEM((n2 * _pitch(n1), cb), F32),
                        pltpu.VMEM((n1 // 2 * _pitch(2 * n2), cb), F32), pltpu.VMEM((n2 * _pitch(n1 // 2), cb), F32)],
        compiler_params=_cparams(2),
        name="longconv",
    )(x0, xv, kf, bias.reshape(1, hw), f1, m2f, m2i, f3)


NA_QROWS = NA_WIN_ROWS // 2
NA_BAND_ROWS = NA_QROWS + NA_WIN_ROWS


def _bias_kernel(rpb_ref, o_ref, *, n_dc):
    h = pl.program_id(0)
    n_dr = 2 * NA_WIN_ROWS - 1
    qcol = lax.broadcasted_iota(jnp.int32, (GRID_W, LANES), 0)
    lane = lax.broadcasted_iota(jnp.int32, (GRID_W, LANES), 1)
    kcol = lane % GRID_W
    win = jnp.clip(qcol - NA_WIN_COLS // 2, 0, GRID_W - NA_WIN_COLS)
    col_ok = (kcol >= win) & (kcol < win + NA_WIN_COLS)
    low_half = lane < GRID_W
    lane_row = lax.broadcasted_iota(jnp.int32, (1, LANES), 1)
    toeplitz = []
    for dr in range(n_dr):
        vec = jnp.zeros((1, LANES), F32)
        for d in range(n_dc):
            vec = jnp.where(lane_row == d, rpb_ref[(h * n_dr + dr) * n_dc + d], vec)
        rows = jnp.broadcast_to(vec * LOG2_E, (GRID_W, LANES))
        toeplitz.append([pltpu.roll(rows, (LANES - (NA_WIN_COLS - 1) + half * GRID_W) % LANES, 1,
                                    stride=1, stride_axis=0) for half in range(2)])
    neg = jnp.full((GRID_W, LANES), NEG_BIG, F32)
    for pos in range(3):
        for p in range(NA_QROWS):
            lo = (0, p, NA_QROWS)[pos]
            for jj in range(NA_BAND_ROWS // 2):
                halves = []
                for half in range(2):
                    j = 2 * jj + half
                    dr = j - p + NA_WIN_ROWS - 1 - NA_QROWS * pos
                    halves.append(toeplitz[dr][half] if lo <= j < lo + NA_WIN_ROWS else neg)
                tile = jnp.where(low_half, halves[0], halves[1])
                o_ref[pos, 0, p * GRID_W:(p + 1) * GRID_W, jj * LANES:(jj + 1) * LANES] = (
                    jnp.where(col_ok, tile, NEG_BIG))


def _bias_table(rpb):
    heads, n_dr, n_dc = rpb.shape
    blk = (NA_QROWS * GRID_W, NA_BAND_ROWS * GRID_W)
    assert 2 * GRID_W == LANES and NA_BAND_ROWS % 2 == 0 and n_dc <= LANES
    return pl.pallas_call(
        functools.partial(_bias_kernel, n_dc=n_dc),
        grid=(heads,),
        in_specs=[pl.BlockSpec(memory_space=pltpu.SMEM)],
        out_specs=pl.BlockSpec((3, 1) + blk, lambda h: (0, h, 0, 0)),
        out_shape=jax.ShapeDtypeStruct((3, heads) + blk, F32),
        compiler_params=_cparams(1),
        name="na_bias",
    )(rpb.reshape(-1))


def _nattn_kernel(q_ref, k_ref, v_ref, kc_ref, vc_ref, bias_ref, o_ref, vs_ref, vcs_ref, *, rows):
    dh = NA_HEAD_DIM
    nq = NA_QROWS * GRID_W
    band = NA_BAND_ROWS * GRID_W
    nt = (((1,), (1,)), ((), ()))

    def own_lanes(n_rows):
        lane = lax.broadcasted_iota(jnp.int32, (n_rows, LANES), 1)
        return [lane < dh, lane >= dh]

    for hh in range(2):
        vs_ref[hh] = jnp.where(own_lanes(v_ref.shape[1])[hh], v_ref[0], 1.0).astype(BF16)
        vcs_ref[hh] = jnp.where(own_lanes(vc_ref.shape[1])[hh], vc_ref[0], 1.0).astype(BF16)
    own_q = own_lanes(nq)
    n_blocks = rows // NA_QROWS

    def per_block(i, carry):
        r0 = i * NA_QROWS
        pos = jnp.where(i == 0, 0, jnp.where(i == n_blocks - 1, 2, 1))
        key0 = pl.multiple_of(jnp.clip(r0 - NA_WIN_ROWS // 2, 0, rows - NA_BAND_ROWS) * GRID_W, GRID_W)
        qrows = pl.ds(pl.multiple_of(r0 * GRID_W, nq), nq)
        q = q_ref[0, qrows, :]
        kb = k_ref[0, pl.ds(key0, band), :]
        outs = []
        for hh in range(2):
            qh = jnp.where(own_q[hh], q, 0.0).astype(BF16)
            s_loc = lax.dot_general(qh, kb, nt, preferred_element_type=F32) + bias_ref[pos, hh]
            s_ctx = lax.dot_general(qh, kc_ref[0], nt, preferred_element_type=F32)
            s = jnp.concatenate([s_loc, s_ctx], axis=-1)
            p = jnp.exp2(s - jnp.max(s, axis=-1, keepdims=True)).astype(BF16)
            o = _dot(p[:, 0:band], vs_ref[hh, pl.ds(key0, band), :]) + _dot(p[:, band:], vcs_ref[hh])
            outs.append(o / pltpu.roll(o, dh, axis=1))
        o_ref[0, qrows, :] = jnp.where(own_q[0], outs[0], outs[1]).astype(BF16)
        return carry
    lax.fori_loop(0, n_blocks, per_block, 0, unroll=NA_UNROLL)


def _nattn(q, k, v, kc, vc, bias):
    b, s, aw = q.shape
    nctx = kc.shape[1]
    hps = LANES // NA_HEAD_DIM
    assert (s // GRID_W) % NA_QROWS == 0 and s // GRID_W >= NA_BAND_ROWS
    tok = pl.BlockSpec((1, s, LANES), lambda j, bb: (bb, 0, j))
    ctx = pl.BlockSpec((1, nctx, LANES), lambda j, bb: (bb, 0, j))
    return pl.pallas_call(
        functools.partial(_nattn_kernel, rows=s // GRID_W),
        grid=(aw // LANES, b),
        in_specs=[tok, tok, tok, ctx, ctx,
                  pl.BlockSpec((3, hps) + bias.shape[2:], lambda j, bb: (0, j, 0, 0))],
        out_specs=tok,
        out_shape=jax.ShapeDtypeStruct((b, s, aw), BF16),
        scratch_shapes=[pltpu.VMEM((hps, s, LANES), BF16), pltpu.VMEM((hps, nctx, LANES), BF16)],
        compiler_params=_cparams(2),
        name="nattn",
    )(q, k, v, kc, vc, bias)


def _outproj_kernel(yh_ref, ya_ref, h_ref, mod_ref, g_ref, w_ref, o_ref, *, d, hw):
    gate = mod_ref[0, :, 2 * d:3 * d]
    y = _dot(yh_ref[0], w_ref[0:hw, :]) + _dot(ya_ref[0], w_ref[hw:, :])
    o_ref[0] = h_ref[0] + gate * _rms(y, g_ref[...])


def _outproj(y_hy, y_na, h, mod, g, w_out, tile):
    b, s, d = h.shape
    hw = y_hy.shape[2]
    aw = y_na.shape[2]
    tok = lambda width: pl.BlockSpec((1, tile, width), lambda bb, i: (bb, i, 0))
    return pl.pallas_call(
        functools.partial(_outproj_kernel, d=d, hw=hw),
        grid=(b, s // tile),
        in_specs=[tok(hw), tok(aw), tok(d),
                  pl.BlockSpec((1, 1, mod.shape[2]), lambda bb, i: (bb, 0, 0)),
                  _const_spec((1, d)), _const_spec(w_out.shape)],
        out_specs=tok(d),
        out_shape=jax.ShapeDtypeStruct((b, s, d), F32),
        compiler_params=_cparams(2),
        name="outproj",
    )(y_hy, y_na, h, mod, g.reshape(1, d), w_out)


def _ffn_kernel(prev_ref, main_ref, next_ref, mod_ref, gpre_ref, gpost_ref, wup_ref, cw_ref, cb_ref, wdn_ref,
                o_ref, hn_ref, z_ref, acc_ref, *, tile, d, n_chunks):
    fc = FFN_CHUNK
    shift = mod_ref[0, :, 3 * d:4 * d]
    scale = mod_ref[0, :, 4 * d:5 * d]
    gate = mod_ref[0, :, 5 * d:6 * d]
    g = gpre_ref[...]
    _fill_halo_tile(hn_ref, prev_ref, main_ref, next_ref,
                    lambda t: _mod_norm(t, g, scale, shift), tile)
    acc_ref[...] = jnp.zeros_like(acc_ref)

    def up(j, slot):
        z_ref[slot] = _dot(hn_ref[...], wup_ref[j])

    def down(j, slot):
        cw = cw_ref[j]
        c = (cw[0:1, :] * z_ref[slot, pl.ds(HALO - 1, tile), :]
             + cw[1:2, :] * z_ref[slot, pl.ds(HALO, tile), :]
             + cw[2:3, :] * z_ref[slot, pl.ds(HALO + 1, tile), :]
             + cb_ref[j])
        a = (_silu(c[:, 0:fc]) * c[:, fc:2 * fc]).astype(BF16)
        acc_ref[...] += _dot(a, wdn_ref[j])

    def step(j, slot):
        up(j + 1, 1 - slot)
        down(j, slot)

    up(0, 0)
    n_pairs = (n_chunks - 1) // 2

    def pair(jj, carry):
        step(2 * jj, 0)
        step(2 * jj + 1, 1)
        return carry
    lax.fori_loop(0, n_pairs, pair, 0)
    if (n_chunks - 1) % 2:
        step(n_chunks - 2, 0)
    down(n_chunks - 1, (n_chunks - 1) % 2)
    o_ref[0] = main_ref[0] + gate * _rms(acc_ref[...], gpost_ref[...])


def _ffn(h, mod, g_pre, g_post, w_up, w_dw, b_dw, w_down, tile):
    b, s, d = h.shape
    dff = w_down.shape[0]
    fc = FFN_CHUNK
    nch = dff // fc
    pair = lambda t: jnp.concatenate([t[..., :dff].reshape(t.shape[:-1] + (nch, fc)),
                                      t[..., dff:].reshape(t.shape[:-1] + (nch, fc))], axis=-1)
    wup = jnp.transpose(pair(w_up), (1, 0, 2)).astype(BF16)
    cw = jnp.transpose(pair(w_dw), (1, 0, 2))
    cb = pair(b_dw).reshape(nch, 1, 2 * fc)
    wdn = w_down.reshape(nch, fc, d).astype(BF16)
    return pl.pallas_call(
        functools.partial(_ffn_kernel, tile=tile, d=d, n_chunks=nch),
        grid=(b, s // tile),
        in_specs=_halo_specs(tile, d, s) + [
            pl.BlockSpec((1, 1, mod.shape[2]), lambda bb, i: (bb, 0, 0)),
            _const_spec((1, d)), _const_spec((1, d)),
            _const_spec(wup.shape), _const_spec(cw.shape), _const_spec(cb.shape), _const_spec(wdn.shape)],
        out_specs=pl.BlockSpec((1, tile, d), lambda bb, i: (bb, i, 0)),
        out_shape=jax.ShapeDtypeStruct((b, s, d), F32),
        scratch_shapes=[pltpu.VMEM((tile + 2 * HALO, d), BF16),
                        pltpu.VMEM((2, tile + 2 * HALO, 2 * fc), F32),
                        pltpu.VMEM((tile, d), F32)],
        compiler_params=_cparams(2),
        name="convffn",
    )(h, h, h, mod, g_pre.reshape(1, d), g_post.reshape(1, d), wup, cw, cb, wdn)


CONF_TOKENS = 64


def _conformer_kernel(prev_ref, main_ref, next_ref, mod_ref, gpre_ref, gpost_ref, w1_ref, b1_ref,
                      dw_ref, db_ref, lg_ref, lb_ref, w2_ref, b2_ref, o_ref, hn_ref, u_ref, c_ref,
                      *, tile, d, seq):
    shift = mod_ref[0, :, 0:d]
    scale = mod_ref[0, :, d:2 * d]
    gate = mod_ref[0, :, 2 * d:3 * d]
    g = gpre_ref[...]
    _fill_halo_tile(hn_ref, prev_ref, main_ref, next_ref,
                    lambda t: _mod_norm(t, g, scale, shift), tile)
    ag = _dot(hn_ref[...], w1_ref[...]) + b1_ref[...]
    pos = (pl.program_id(1) * tile - HALO
           + lax.broadcasted_iota(jnp.int32, (tile + 2 * HALO, 1), 0))
    u = jnp.where((pos >= 0) & (pos < seq), ag[:, 0:d] * jax.nn.sigmoid(ag[:, d:2 * d]), 0.0)
    groups = d // LANES
    n_rows = tile + 2 * HALO
    for j in range(groups):
        u_ref[pl.ds(j, n_rows, stride=groups), :] = u[:, j * LANES:(j + 1) * LANES]
    first = HALO - CONF_K // 2

    def token_tile(ref, t):
        return ref[pl.ds(pl.multiple_of(t * groups, groups), groups), :]

    def conv_block(tb, carry):
        t0 = tb * CONF_TOKENS
        for i in range(CONF_TOKENS):
            acc = db_ref[...]
            for k in range(CONF_K):
                acc = acc + dw_ref[k] * token_tile(u_ref, t0 + i + first + k)
            c_ref[pl.ds(pl.multiple_of((t0 + i) * groups, groups), groups), :] = acc
        return carry
    lax.fori_loop(0, tile // CONF_TOKENS, conv_block, 0)
    c = jnp.concatenate([c_ref[pl.ds(j, tile, stride=groups), :] for j in range(groups)], axis=1)
    mu = jnp.mean(c, axis=-1, keepdims=True)
    cc = c - mu
    var = jnp.mean(cc * cc, axis=-1, keepdims=True)
    ln = cc * lax.rsqrt(var + EPS) * lg_ref[...] + lb_ref[...]
    y = _dot(_silu(ln).astype(BF16), w2_ref[...]) + b2_ref[...]
    o_ref[0] = main_ref[0] + gate * _rms(y, gpost_ref[...])


def _conformer(h, mod, g_pre, g_post, w_pw1, b_pw1, w_dw, b_dw, ln_g, ln_b, w_pw2, b_pw2, tile):
    b, s, d = h.shape
    groups = d // LANES
    assert groups == F32_SUBLANES and tile % CONF_TOKENS == 0
    row = lambda t: t.reshape(1, -1)
    return pl.pallas_call(
        functools.partial(_conformer_kernel, tile=tile, d=d, seq=s),
        grid=(b, s // tile),
        in_specs=_halo_specs(tile, d, s) + [
            pl.BlockSpec((1, 1, mod.shape[2]), lambda bb, i: (bb, 0, 0)),
            _const_spec((1, d)), _const_spec((1, d)),
            _const_spec(w_pw1.shape), _const_spec((1, 2 * d)),
            _const_spec((CONF_K, groups, LANES)), _const_spec((groups, LANES)),
            _const_spec((1, d)), _const_spec((1, d)),
            _const_spec(w_pw2.shape), _const_spec((1, d))],
        out_specs=pl.BlockSpec((1, tile, d), lambda bb, i: (bb, i, 0)),
        out_shape=jax.ShapeDtypeStruct((b, s, d), F32),
        scratch_shapes=[pltpu.VMEM((tile + 2 * HALO, d), BF16),
                        pltpu.VMEM(((tile + 2 * HALO) * groups, LANES), F32),
                        pltpu.VMEM((tile * groups, LANES), F32)],
        compiler_params=_cparams(2),
        name="conformer",
    )(h, h, h, mod, row(g_pre), row(g_post), w_pw1.astype(BF16), row(b_pw1),
      w_dw.reshape(CONF_K, groups, LANES), b_dw.reshape(groups, LANES),
      row(ln_g), row(ln_b), w_pw2.astype(BF16), row(b_pw2))


def _token_tile(seq):
    return min(seq, 1024)


def kernel(x, c, ctx, c_ctx, w_mod, b_mod, g_mix_pre, g_mix_post, g_ffn_pre, g_ffn_post, w_in, w_out, hy_short_w, hy_short_b, hy_f_w1, hy_f_b1, hy_f_w2, hy_f_b2, hy_f_w3, hy_f_b3, hy_f_w4, hy_f_freq, hy_bias, na_rpb, cf_w_pw1, cf_b_pw1, cf_w_dw, cf_b_dw, cf_ln_g, cf_ln_b, cf_w_pw2, cf_b_pw2, ffn_w_up, ffn_w_dw, ffn_b_dw, ffn_w_down):
    bsz, seq, d = x.shape
    depth = w_mod.shape[0]
    hw = hy_bias.shape[1]
    aw = NA_HEADS * NA_HEAD_DIM
    tile = _token_tile(seq)
    assert seq % tile == 0 and seq % (GRID_W * NA_WIN_ROWS) == 0 and tile % HALO == 0

    n_rows = -(-(bsz + 1) // BF16_SUBLANES) * BF16_SUBLANES
    cc = jnp.concatenate([c, c_ctx[None, :], jnp.zeros((n_rows - bsz - 1, d), F32)], axis=0)
    mod_all = _adaln(cc, w_mod, b_mod)

    h = x
    for layer in range(depth):
        mod = mod_all[layer, :bsz].reshape(bsz, 1, 6 * d)
        if layer % 2 == 0:
            e = layer // 2
            w_e = w_in[e].astype(BF16)
            k_ctx, v_ctx = _ctxkv(ctx, mod_all[layer, bsz:bsz + 1, :2 * d], g_mix_pre[layer],
                                  w_e[:, 3 * hw + aw:])
            x0, xv, q, k, v = _inproj(h, mod, g_mix_pre[layer], w_e[:, :3 * hw], w_e[:, 3 * hw:],
                                      hy_short_w[e], hy_short_b[e], tile)
            f1, m2f, m2i, f3 = _dft_tables(seq)
            hf, hb = _hyena_filters(seq, hy_f_w1[e], hy_f_b1[e], hy_f_w2[e], hy_f_b2[e],
                                    hy_f_w3[e], hy_f_b3[e], hy_f_w4[e], hy_f_freq[e])
            kf = _filter_spectrum(hf, hb, f1, m2f)
            y_hy = _longconv(x0, xv, kf, hy_bias[e], f1, m2f, m2i, f3)
            y_na = _nattn(q, k, v, k_ctx, v_ctx, _bias_table(na_rpb[e]))
            h = _outproj(y_hy, y_na, h, mod, g_mix_post[layer], w_out[e].astype(BF16), tile)
        else:
            o = layer // 2
            h = _conformer(h, mod, g_mix_pre[layer], g_mix_post[layer], cf_w_pw1[o], cf_b_pw1[o],
                           cf_w_dw[o], cf_b_dw[o], cf_ln_g[o], cf_ln_b[o], cf_w_pw2[o], cf_b_pw2[o], tile)
        h = _ffn(h, mod, g_ffn_pre[layer], g_ffn_post[layer], ffn_w_up[layer], ffn_w_dw[layer],
                 ffn_b_dw[layer], ffn_w_down[layer], tile)
    return h
```

```python
import functools
import math

import numpy as np
import jax
import jax.numpy as jnp
from jax import lax
from jax.experimental import pallas as pl
from jax.experimental.pallas import tpu as pltpu

F32 = jnp.float32
BF16 = jnp.bfloat16

EPS = 1e-6
NEG_BIG = -1e30
LOG2_E = math.log2(math.e)

GRID_W = 64
NA_HEADS = 8
NA_HEAD_DIM = 64
NA_WIN_ROWS = 8
NA_WIN_COLS = 16
HY_SHORT_K = 3
HY_EMB_DIM = 33
HY_DECAY_FAST = 0.3
HY_DECAY_SLOW = 1.5
HY_DECAY_TARGET = 1e-2
CONF_K = 31
FFN_CONV_K = 3

LANES = 128
F32_SUBLANES = 8
BF16_SUBLANES = 16
VMEM_LIMIT = 56 * 1024 * 1024

HALO = BF16_SUBLANES
DFT_N2 = 64
DFT_UNROLL = 16
NA_UNROLL = 4
FFN_CHUNK = 256
EMB_PAD = LANES


def _cparams(n_axes):
    return pltpu.CompilerParams(dimension_semantics=("arbitrary",) * n_axes,
                                vmem_limit_bytes=VMEM_LIMIT)


def _const_spec(shape):
    nd = len(shape)
    return pl.BlockSpec(shape, lambda *_: (0,) * nd, pipeline_mode=pl.Buffered(1))


def _silu(t):
    return t * jax.nn.sigmoid(t)


def _mod_norm(t, g, scale, shift):
    ms = jnp.mean(t * t, axis=-1, keepdims=True)
    return (t * lax.rsqrt(ms + EPS) * g) * (1.0 + scale) + shift


def _rms(t, g):
    ms = jnp.mean(t * t, axis=-1, keepdims=True)
    return t * lax.rsqrt(ms + EPS) * g


def _dot(a, b):
    return jnp.dot(a, b, preferred_element_type=F32)


def _dot_hi(a, b):
    return jnp.dot(a, b, preferred_element_type=F32, precision=lax.Precision.HIGHEST)


def _fill_halo_tile(hn_ref, prev_ref, main_ref, next_ref, norm_fn, tile):
    i = pl.program_id(1)
    last = pl.num_programs(1) - 1
    p = norm_fn(prev_ref[0])
    n = norm_fn(next_ref[0])
    hn_ref[0:HALO, :] = jnp.where(i > 0, p, 0.0).astype(BF16)
    hn_ref[HALO:HALO + tile, :] = norm_fn(main_ref[0]).astype(BF16)
    hn_ref[HALO + tile:, :] = jnp.where(i < last, n, 0.0).astype(BF16)


def _halo_specs(tile, d, seq):
    r = tile // HALO
    nblk = seq // HALO
    return [
        pl.BlockSpec((1, HALO, d), lambda b, i: (b, jnp.maximum(i * r - 1, 0), 0)),
        pl.BlockSpec((1, tile, d), lambda b, i: (b, i, 0)),
        pl.BlockSpec((1, HALO, d), lambda b, i: (b, jnp.minimum((i + 1) * r, nblk - 1), 0)),
    ]


def _adaln_kernel(c_ref, w_ref, b_ref, o_ref):
    s = _silu(c_ref[...]).astype(BF16)
    o_ref[0] = _dot(s, w_ref[0].astype(BF16)) + b_ref[0]


def _adaln(cc, w_mod, b_mod):
    depth, d, n6 = w_mod.shape
    r = cc.shape[0]
    nb = d
    return pl.pallas_call(
        _adaln_kernel,
        grid=(depth, n6 // nb),
        in_specs=[pl.BlockSpec((r, d), lambda l, j: (0, 0)),
                  pl.BlockSpec((1, d, nb), lambda l, j: (l, 0, j)),
                  pl.BlockSpec((1, 1, nb), lambda l, j: (l, 0, j))],
        out_specs=pl.BlockSpec((1, r, nb), lambda l, j: (l, 0, j)),
        out_shape=jax.ShapeDtypeStruct((depth, r, n6), F32),
        compiler_params=_cparams(2),
        name="adaln",
    )(cc, w_mod, b_mod.reshape(depth, 1, n6))


def _inproj_kernel(prev_ref, main_ref, next_ref, mod_ref, g_ref, why_ref, wqkv_ref, sw_ref, sb_ref,
                   x0_ref, xv_ref, q_ref, k_ref, v_ref, hn_ref, z_ref, *, tile, d, hw, aw):
    shift = mod_ref[0, :, 0:d]
    scale = mod_ref[0, :, d:2 * d]
    g = g_ref[...]
    _fill_halo_tile(hn_ref, prev_ref, main_ref, next_ref,
                    lambda t: _mod_norm(t, g, scale, shift), tile)
    z_ref[...] = _dot(hn_ref[...], why_ref[...])
    uc = (sw_ref[0:1, :] * z_ref[pl.ds(HALO - 1, tile), :]
          + sw_ref[1:2, :] * z_ref[pl.ds(HALO, tile), :]
          + sw_ref[2:3, :] * z_ref[pl.ds(HALO + 1, tile), :]
          + sb_ref[...])
    x0_ref[0] = uc[:, 0:hw]
    xv_ref[0] = uc[:, hw:2 * hw] * uc[:, 2 * hw:3 * hw]
    qkv = _dot(hn_ref[pl.ds(HALO, tile), :], wqkv_ref[...])
    q_ref[0] = (qkv[:, 0:aw] * (NA_HEAD_DIM ** -0.5 * LOG2_E)).astype(BF16)
    k_ref[0] = qkv[:, aw:2 * aw].astype(BF16)
    v_ref[0] = qkv[:, 2 * aw:3 * aw].astype(BF16)


def _inproj(h, mod, g, w_hy, w_qkv, short_w, short_b, tile):
    b, s, d = h.shape
    hw = w_hy.shape[1] // 3
    aw = w_qkv.shape[1] // 3
    tok = lambda width: pl.BlockSpec((1, tile, width), lambda bb, i: (bb, i, 0))
    return pl.pallas_call(
        functools.partial(_inproj_kernel, tile=tile, d=d, hw=hw, aw=aw),
        grid=(b, s // tile),
        in_specs=_halo_specs(tile, d, s) + [
            pl.BlockSpec((1, 1, mod.shape[2]), lambda bb, i: (bb, 0, 0)),
            _const_spec((1, d)),
            _const_spec(w_hy.shape),
            _const_spec(w_qkv.shape),
            _const_spec(short_w.shape),
            _const_spec((1, 3 * hw)),
        ],
        out_specs=[tok(hw), tok(hw), tok(aw), tok(aw), tok(aw)],
        out_shape=[jax.ShapeDtypeStruct((b, s, hw), F32), jax.ShapeDtypeStruct((b, s, hw), F32),
                   jax.ShapeDtypeStruct((b, s, aw), BF16), jax.ShapeDtypeStruct((b, s, aw), BF16),
                   jax.ShapeDtypeStruct((b, s, aw), BF16)],
        scratch_shapes=[pltpu.VMEM((tile + 2 * HALO, d), BF16),
                        pltpu.VMEM((tile + 2 * HALO, 3 * hw), F32)],
        compiler_params=_cparams(2),
        name="inproj",
    )(h, h, h, mod, g.reshape(1, d), w_hy, w_qkv, short_w, short_b.reshape(1, 3 * hw))


def _ctxkv_kernel(ctx_ref, mod_ref, g_ref, w_ref, k_ref, v_ref, *, d, aw):
    shift = mod_ref[:, 0:d]
    scale = mod_ref[:, d:2 * d]
    cn = _mod_norm(ctx_ref[0], g_ref[...], scale, shift).astype(BF16)
    kv = _dot(cn, w_ref[...])
    k_ref[0] = kv[:, 0:aw].astype(BF16)
    v_ref[0] = kv[:, aw:2 * aw].astype(BF16)


def _ctxkv(ctx, mod_ctx, g, w_kv):
    b, n, d = ctx.shape
    aw = w_kv.shape[1] // 2
    out = pl.BlockSpec((1, n, aw), lambda bb: (bb, 0, 0))
    return pl.pallas_call(
        functools.partial(_ctxkv_kernel, d=d, aw=aw),
        grid=(b,),
        in_specs=[pl.BlockSpec((1, n, d), lambda bb: (bb, 0, 0)),
                  _const_spec(mod_ctx.shape), _const_spec((1, d)), _const_spec(w_kv.shape)],
        out_specs=[out, out],
        out_shape=[jax.ShapeDtypeStruct((b, n, aw), BF16)] * 2,
        compiler_params=_cparams(1),
        name="ctxkv",
    )(ctx, mod_ctx, g.reshape(1, d), w_kv)


def _filter_kernel(z_ref, w1_ref, b1_ref, w2_ref, b2_ref, w3_ref, b3_ref, w4_ref, fr_ref, dl_ref,
                   hf_ref, hb_ref, *, rows, hw):
    half = rows // 2
    z = jnp.concatenate([z_ref[0:half, :], z_ref[half:rows, :]], axis=1)
    hdn = jnp.sin(fr_ref[0:1, :] * (_dot_hi(z, w1_ref[...]) + b1_ref[...]))
    hdn = jnp.sin(fr_ref[1:2, :] * (_dot_hi(hdn, w2_ref[...]) + b2_ref[...]))
    hdn = jnp.sin(fr_ref[2:3, :] * (_dot_hi(hdn, w3_ref[...]) + b3_ref[...]))
    filt = _dot_hi(hdn, w4_ref[...])
    for part in range(2):
        rs = slice(part * half, (part + 1) * half)
        f = filt[:, part * 2 * hw:(part + 1) * 2 * hw]
        decay = jnp.exp(-z_ref[rs, 0:1] * dl_ref[...])
        hf_ref[rs, :] = f[:, 0:hw] * decay
        pos = pl.program_id(0) * rows + part * half + lax.broadcasted_iota(jnp.int32, (half, 1), 0)
        hb_ref[rs, :] = jnp.where(pos > 0, f[:, hw:2 * hw] * decay, 0.0)


def _hyena_filters(length, w1, b1, w2, b2, w3, b3, w4, freq):
    hid = w1.shape[1]
    hw = w4.shape[1] // 2
    bands = (HY_EMB_DIM - 1) // 2
    t01 = np.linspace(0.0, 1.0, length)[:, None]
    ang = (2.0 * np.pi * np.arange(length) / length)[:, None] * np.linspace(1e-4, bands - 1, bands)[None, :]
    z = jnp.asarray(np.concatenate([t01, np.cos(ang), -np.sin(ang),
                                    np.zeros((length, EMB_PAD - HY_EMB_DIM))], axis=-1), F32)
    w1p = jnp.concatenate([w1, jnp.zeros((EMB_PAD - HY_EMB_DIM, hid), F32)], axis=0)
    deltas = jnp.asarray(np.abs(np.linspace(math.log(HY_DECAY_TARGET) / HY_DECAY_FAST,
                                            math.log(HY_DECAY_TARGET) / HY_DECAY_SLOW, hw))[None, :], F32)
    rows = min(length, 1024)
    out = pl.BlockSpec((rows, hw), lambda i: (i, 0))
    twice = lambda v: jnp.concatenate([v, v], axis=-1)
    diag2 = lambda w: jnp.concatenate([jnp.concatenate([w, jnp.zeros_like(w)], axis=1),
                                       jnp.concatenate([jnp.zeros_like(w), w], axis=1)], axis=0)
    args = (z, diag2(w1p), twice(b1.reshape(1, hid)), diag2(w2), twice(b2.reshape(1, hid)),
            diag2(w3), twice(b3.reshape(1, hid)), diag2(w4), twice(freq), deltas)
    return pl.pallas_call(
        functools.partial(_filter_kernel, rows=rows, hw=hw),
        grid=(length // rows,),
        in_specs=[pl.BlockSpec((rows, EMB_PAD), lambda i: (i, 0))] + [_const_spec(a.shape) for a in args[1:]],
        out_specs=[out, out],
        out_shape=[jax.ShapeDtypeStruct((length, hw), F32)] * 2,
        compiler_params=_cparams(1),
        name="hyena_filter",
    )(*args)


def _dft_tables(seq):
    n2 = DFT_N2
    h1 = seq // n2
    n1 = 2 * h1
    hk = n1 // 2
    n = n1 * n2
    k1 = np.arange(hk)[:, None]
    t1 = np.arange(h1)[None, :]
    a1 = 2.0 * np.pi * ((k1 * t1) % n1) / n1
    nyquist = np.where(t1 % 2 == 0, 1.0, -1.0)
    twice = np.where(k1 == 0, 1.0, 2.0)
    f1_im, f3_im = -np.sin(a1), -twice * np.sin(a1)
    f1_im[0:1], f3_im[0:1] = nyquist, nyquist
    f1 = np.concatenate([np.cos(a1), f1_im], axis=0)
    f3 = np.concatenate([(twice * np.cos(a1)).T, f3_im.T], axis=1) / n
    kk1 = np.arange(hk + 1)[:, None, None]
    k2 = np.arange(n2)[None, :, None]
    t2 = np.arange(n2)[None, None, :]
    th = 2.0 * np.pi * ((t2 * k2 * n1 + t2 * kk1) % n) / n
    wr, wi = np.cos(th), -np.sin(th)
    m2f = np.concatenate([np.concatenate([wr, -wi], axis=2),
                          np.concatenate([wi, wr], axis=2)], axis=1)
    m2i = np.transpose(m2f, (0, 2, 1))
    as_bf16 = lambda a: jnp.asarray(a, F32).astype(BF16)
    return as_bf16(f1), as_bf16(m2f), as_bf16(m2i), as_bf16(f3)


def _pitch(rows):
    return rows + F32_SUBLANES


def _dft_stage1(load_rows, f1_ref, a_ref, n1):
    def body(t2, carry):
        p = _dot(f1_ref[...], load_rows(t2).astype(BF16))
        a_ref[pl.ds(pl.multiple_of(t2 * _pitch(n1), F32_SUBLANES), n1), :] = p
        return carry
    lax.fori_loop(0, DFT_N2, body, 0, unroll=DFT_UNROLL)


def _dft_stage2(a_ref, m2f_ref, k1, n1):
    hk = n1 // 2
    if k1 is None:
        ar = a_ref[pl.ds(hk, DFT_N2, stride=_pitch(n1)), :]
        ai = jnp.zeros_like(ar)
        k1 = hk
    else:
        ar = a_ref[pl.ds(k1, DFT_N2, stride=_pitch(n1)), :]
        ai = jnp.where(k1 == 0, 0.0, a_ref[pl.ds(hk + k1, DFT_N2, stride=_pitch(n1)), :])
    return _dot(m2f_ref[k1], jnp.concatenate([ar, ai], axis=0).astype(BF16))


def _spectrum_kernel(hf_ref, hb_ref, f1_ref, m2f_ref, kf_ref, af_ref, ab_ref, *, n1):
    n2 = DFT_N2
    hk = n1 // 2
    _dft_stage1(lambda t2: hf_ref[pl.ds(t2, hk, stride=n2), :], f1_ref, af_ref, n1)
    _dft_stage1(lambda t2: hb_ref[pl.ds(t2, hk, stride=n2), :], f1_ref, ab_ref, n1)

    def spectrum(k1):
        xf = _dft_stage2(af_ref, m2f_ref, k1, n1)
        xb = _dft_stage2(ab_ref, m2f_ref, k1, n1)
        sign = jnp.where(lax.broadcasted_iota(jnp.int32, (2 * n2, 1), 0) < n2, 1.0, -1.0)
        return xf + sign * xb

    def body(k1, carry):
        kf_ref[k1] = spectrum(k1)
        return carry
    lax.fori_loop(0, hk, body, 0, unroll=DFT_UNROLL)
    kf_ref[hk] = spectrum(None)


def _filter_spectrum(hf, hb, f1, m2f):
    s, hw = hf.shape
    n2 = DFT_N2
    n1 = 2 * s // n2
    cb = LANES
    col = pl.BlockSpec((s, cb), lambda j: (0, j))
    return pl.pallas_call(
        functools.partial(_spectrum_kernel, n1=n1),
        grid=(hw // cb,),
        in_specs=[col, col, _const_spec(f1.shape), _const_spec(m2f.shape)],
        out_specs=pl.BlockSpec((n1 // 2 + 1, 2 * n2, cb), lambda j: (0, 0, j)),
        out_shape=jax.ShapeDtypeStruct((n1 // 2 + 1, 2 * n2, hw), F32),
        scratch_shapes=[pltpu.VMEM((n2 * _pitch(n1), cb), F32), pltpu.VMEM((n2 * _pitch(n1), cb), F32)],
        compiler_params=_cparams(1),
        name="filter_spectrum",
    )(hf, hb, f1, m2f)


def _longconv_kernel(x0_ref, xv_ref, kf_ref, bias_ref, f1_ref, m2f_ref, m2i_ref, f3_ref, o_ref,
                     xs_ref, a_ref, b_ref, y_ref, *, n1):
    n2 = DFT_N2
    h1 = hk = n1 // 2

    def pitch_rows(t1, carry):
        xs_ref[pl.ds(pl.multiple_of(t1 * _pitch(n2), F32_SUBLANES), n2), :] = (
            xv_ref[0, pl.ds(pl.multiple_of(t1 * n2, n2), n2), :])
        return carry
    lax.fori_loop(0, h1, pitch_rows, 0, unroll=DFT_UNROLL)
    _dft_stage1(lambda t2: xs_ref[pl.ds(t2, h1, stride=_pitch(n2)), :], f1_ref, a_ref, n1)

    def filtered(k1, row):
        x = _dft_stage2(a_ref, m2f_ref, k1, n1)
        kf = kf_ref[row]
        xr, xi = x[0:n2], x[n2:2 * n2]
        kr, ki = kf[0:n2], kf[n2:2 * n2]
        y = jnp.concatenate([xr * kr - xi * ki, xr * ki + xi * kr], axis=0).astype(BF16)
        return _dot(m2i_ref[row], y)

    def per_k1(k1, carry):
        b_ref[pl.ds(pl.multiple_of(k1 * _pitch(2 * n2), F32_SUBLANES), 2 * n2), :] = filtered(k1, k1)
        return carry
    lax.fori_loop(0, hk, per_k1, 0, unroll=DFT_UNROLL)
    b_ref[pl.ds(n2, n2), :] = filtered(None, hk)[0:n2]

    def per_t2(t2, carry):
        br = b_ref[pl.ds(t2, hk, stride=_pitch(2 * n2)), :]
        bi = b_ref[pl.ds(n2 + t2, hk, stride=_pitch(2 * n2)), :]
        y = _dot(f3_ref[...], jnp.concatenate([br, bi], axis=0).astype(BF16))
        y_ref[pl.ds(pl.multiple_of(t2 * _pitch(h1), F32_SUBLANES), h1), :] = y
        return carry
    lax.fori_loop(0, n2, per_t2, 0, unroll=DFT_UNROLL)

    def per_t1(t1, carry):
        rows = pl.ds(pl.multiple_of(t1 * n2, n2), n2)
        y = y_ref[pl.ds(t1, n2, stride=_pitch(h1)), :]
        o_ref[0, rows, :] = (x0_ref[0, rows, :] * (y + xv_ref[0, rows, :] * bias_ref[...])).astype(BF16)
        return carry
    lax.fori_loop(0, h1, per_t1, 0, unroll=DFT_UNROLL)


def _longconv(x0, xv, kf, bias, f1, m2f, m2i, f3):
    b, s, hw = x0.shape
    n2 = DFT_N2
    n1 = 2 * s // n2
    cb = LANES
    tok = pl.BlockSpec((1, s, cb), lambda j, bb: (bb, 0, j))
    return pl.pallas_call(
        functools.partial(_longconv_kernel, n1=n1),
        grid=(hw // cb, b),
        in_specs=[tok, tok,
                  pl.BlockSpec((n1 // 2 + 1, 2 * n2, cb), lambda j, bb: (0, 0, j), pipeline_mode=pl.Buffered(1)),
                  pl.BlockSpec((1, cb), lambda j, bb: (0, j)),
                  _const_spec(f1.shape), _const_spec(m2f.shape), _const_spec(m2i.shape),
                  _const_spec(f3.shape)],
        out_specs=tok,
        out_shape=jax.ShapeDtypeStruct((b, s, hw), BF16),
        scratch_shapes=[pltpu.VMEM((n1 // 2 * _pitch(n2), cb), F32), pltpu.VMEM((n2 * _pitch(n1), cb), F32),
                        pltpu.VMEM((n1 // 2 * _pitch(2 * n2), cb), F32), pltpu.VMEM((n2 * _pitch(n1 // 2), cb), F32)],
        compiler_params=_cparams(2),
        name="longconv",
    )(x0, xv, kf, bias.reshape(1, hw), f1, m2f, m2i, f3)


NA_QROWS = NA_WIN_ROWS // 2
NA_BAND_ROWS = NA_QROWS + NA_WIN_ROWS


def _bias_kernel(rpb_ref, o_ref, *, n_dc):
    h = pl.program_id(0)
    n_dr = 2 * NA_WIN_ROWS - 1
    qcol = lax.broadcasted_iota(jnp.int32, (GRID_W, LANES), 0)
    lane = lax.broadcasted_iota(jnp.int32, (GRID_W, LANES), 1)
    kcol = lane % GRID_W
    win = jnp.clip(qcol - NA_WIN_COLS // 2, 0, GRID_W - NA_WIN_COLS)
    col_ok = (kcol >= win) & (kcol < win + NA_WIN_COLS)
    low_half = lane < GRID_W
    lane_row = lax.broadcasted_iota(jnp.int32, (1, LANES), 1)
    toeplitz = []
    for dr in range(n_dr):
        vec = jnp.zeros((1, LANES), F32)
        for d in range(n_dc):
            vec = jnp.where(lane_row == d, rpb_ref[(h * n_dr + dr) * n_dc + d], vec)
        rows = jnp.broadcast_to(vec * LOG2_E, (GRID_W, LANES))
        toeplitz.append([pltpu.roll(rows, (LANES - (NA_WIN_COLS - 1) + half * GRID_W) % LANES, 1,
                                    stride=1, stride_axis=0) for half in range(2)])
    neg = jnp.full((GRID_W, LANES), NEG_BIG, F32)
    for pos in range(3):
        for p in range(NA_QROWS):
            lo = (0, p, NA_QROWS)[pos]
            for jj in range(NA_BAND_ROWS // 2):
                halves = []
                for half in range(2):
                    j = 2 * jj + half
                    dr = j - p + NA_WIN_ROWS - 1 - NA_QROWS * pos
                    halves.append(toeplitz[dr][half] if lo <= j < lo + NA_WIN_ROWS else neg)
                tile = jnp.where(low_half, halves[0], halves[1])
                o_ref[pos, 0, p * GRID_W:(p + 1) * GRID_W, jj * LANES:(jj + 1) * LANES] = (
                    jnp.where(col_ok, tile, NEG_BIG))


def _bias_table(rpb):
    heads, n_dr, n_dc = rpb.shape
    blk = (NA_QROWS * GRID_W, NA_BAND_ROWS * GRID_W)
    assert 2 * GRID_W == LANES and NA_BAND_ROWS % 2 == 0 and n_dc <= LANES
    return pl.pallas_call(
        functools.partial(_bias_kernel, n_dc=n_dc),
        grid=(heads,),
        in_specs=[pl.BlockSpec(memory_space=pltpu.SMEM)],
        out_specs=pl.BlockSpec((3, 1) + blk, lambda h: (0, h, 0, 0)),
        out_shape=jax.ShapeDtypeStruct((3, heads) + blk, F32),
        compiler_params=_cparams(1),
        name="na_bias",
    )(rpb.reshape(-1))


def _nattn_kernel(q_ref, k_ref, v_ref, kc_ref, vc_ref, bias_ref, o_ref, vs_ref, vcs_ref, *, rows):
    dh = NA_HEAD_DIM
    nq = NA_QROWS * GRID_W
    band = NA_BAND_ROWS * GRID_W
    nt = (((1,), (1,)), ((), ()))

    def own_lanes(n_rows):
        lane = lax.broadcasted_iota(jnp.int32, (n_rows, LANES), 1)
        return [lane < dh, lane >= dh]

    for hh in range(2):
        vs_ref[hh] = jnp.where(own_lanes(v_ref.shape[1])[hh], v_ref[0], 1.0).astype(BF16)
        vcs_ref[hh] = jnp.where(own_lanes(vc_ref.shape[1])[hh], vc_ref[0], 1.0).astype(BF16)
    own_q = own_lanes(nq)
    n_blocks = rows // NA_QROWS

    def per_block(i, carry):
        r0 = i * NA_QROWS
        pos = jnp.where(i == 0, 0, jnp.where(i == n_blocks - 1, 2, 1))
        key0 = pl.multiple_of(jnp.clip(r0 - NA_WIN_ROWS // 2, 0, rows - NA_BAND_ROWS) * GRID_W, GRID_W)
        qrows = pl.ds(pl.multiple_of(r0 * GRID_W, nq), nq)
        q = q_ref[0, qrows, :]
        kb = k_ref[0, pl.ds(key0, band), :]
        outs = []
        for hh in range(2):
            qh = jnp.where(own_q[hh], q, 0.0).astype(BF16)
            s_loc = lax.dot_general(qh, kb, nt, preferred_element_type=F32) + bias_ref[pos, hh]
            s_ctx = lax.dot_general(qh, kc_ref[0], nt, preferred_element_type=F32)
            s = jnp.concatenate([s_loc, s_ctx], axis=-1)
            p = jnp.exp2(s - jnp.max(s, axis=-1, keepdims=True)).astype(BF16)
            o = _dot(p[:, 0:band], vs_ref[hh, pl.ds(key0, band), :]) + _dot(p[:, band:], vcs_ref[hh])
            outs.append(o / pltpu.roll(o, dh, axis=1))
        o_ref[0, qrows, :] = jnp.where(own_q[0], outs[0], outs[1]).astype(BF16)
        return carry
    lax.fori_loop(0, n_blocks, per_block, 0, unroll=NA_UNROLL)


def _nattn(q, k, v, kc, vc, bias):
    b, s, aw = q.shape
    nctx = kc.shape[1]
    hps = LANES // NA_HEAD_DIM
    assert (s // GRID_W) % NA_QROWS == 0 and s // GRID_W >= NA_BAND_ROWS
    tok = pl.BlockSpec((1, s, LANES), lambda j, bb: (bb, 0, j))
    ctx = pl.BlockSpec((1, nctx, LANES), lambda j, bb: (bb, 0, j))
    return pl.pallas_call(
        functools.partial(_nattn_kernel, rows=s // GRID_W),
        grid=(aw // LANES, b),
        in_specs=[tok, tok, tok, ctx, ctx,
                  pl.BlockSpec((3, hps) + bias.shape[2:], lambda j, bb: (0, j, 0, 0))],
        out_specs=tok,
        out_shape=jax.ShapeDtypeStruct((b, s, aw), BF16),
        scratch_shapes=[pltpu.VMEM((hps, s, LANES), BF16), pltpu.VMEM((hps, nctx, LANES), BF16)],
        compiler_params=_cparams(2),
        name="nattn",
    )(q, k, v, kc, vc, bias)


def _outproj_kernel(yh_ref, ya_ref, h_ref, mod_ref, g_ref, w_ref, o_ref, *, d, hw):
    gate = mod_ref[0, :, 2 * d:3 * d]
    y = _dot(yh_ref[0], w_ref[0:hw, :]) + _dot(ya_ref[0], w_ref[hw:, :])
    o_ref[0] = h_ref[0] + gate * _rms(y, g_ref[...])


def _outproj(y_hy, y_na, h, mod, g, w_out, tile):
    b, s, d = h.shape
    hw = y_hy.shape[2]
    aw = y_na.shape[2]
    tok = lambda width: pl.BlockSpec((1, tile, width), lambda bb, i: (bb, i, 0))
    return pl.pallas_call(
        functools.partial(_outproj_kernel, d=d, hw=hw),
        grid=(b, s // tile),
        in_specs=[tok(hw), tok(aw), tok(d),
                  pl.BlockSpec((1, 1, mod.shape[2]), lambda bb, i: (bb, 0, 0)),
                  _const_spec((1, d)), _const_spec(w_out.shape)],
        out_specs=tok(d),
        out_shape=jax.ShapeDtypeStruct((b, s, d), F32),
        compiler_params=_cparams(2),
        name="outproj",
    )(y_hy, y_na, h, mod, g.reshape(1, d), w_out)


def _ffn_kernel(prev_ref, main_ref, next_ref, mod_ref, gpre_ref, gpost_ref, wup_ref, cw_ref, cb_ref, wdn_ref,
                o_ref, hn_ref, z_ref, acc_ref, *, tile, d, n_chunks):
    fc = FFN_CHUNK
    shift = mod_ref[0, :, 3 * d:4 * d]
    scale = mod_ref[0, :, 4 * d:5 * d]
    gate = mod_ref[0, :, 5 * d:6 * d]
    g = gpre_ref[...]
    _fill_halo_tile(hn_ref, prev_ref, main_ref, next_ref,
                    lambda t: _mod_norm(t, g, scale, shift), tile)
    acc_ref[...] = jnp.zeros_like(acc_ref)

    def chunk_start(j, base):
        start = base + j * fc
        return start if isinstance(j, int) else pl.multiple_of(start, LANES)

    def up(j, slot):
        z_ref[slot, :, 0:fc] = _dot(hn_ref[...], wup_ref[:, pl.ds(chunk_start(j, 0), fc)])
        z_ref[slot, :, fc:2 * fc] = _dot(hn_ref[...], wup_ref[:, pl.ds(chunk_start(j, n_chunks * fc), fc)])

    def down(j, slot):
        cw = cw_ref[j]
        c = (cw[0:1, :] * z_ref[slot, pl.ds(HALO - 1, tile), :]
             + cw[1:2, :] * z_ref[slot, pl.ds(HALO, tile), :]
             + cw[2:3, :] * z_ref[slot, pl.ds(HALO + 1, tile), :]
             + cb_ref[j])
        a = (_silu(c[:, 0:fc]) * c[:, fc:2 * fc]).astype(BF16)
        acc_ref[...] += _dot(a, wdn_ref[pl.ds(chunk_start(j, 0), fc), :])

    def step(j, slot):
        up(j + 1, 1 - slot)
        down(j, slot)

    up(0, 0)
    n_pairs = (n_chunks - 1) // 2

    def pair(jj, carry):
        step(2 * jj, 0)
        step(2 * jj + 1, 1)
        return carry
    lax.fori_loop(0, n_pairs, pair, 0)
    if (n_chunks - 1) % 2:
        step(n_chunks - 2, 0)
    down(n_chunks - 1, (n_chunks - 1) % 2)
    o_ref[0] = main_ref[0] + gate * _rms(acc_ref[...], gpost_ref[...])


def _ffn(h, mod, g_pre, g_post, w_up, w_dw, b_dw, w_down, tile):
    b, s, d = h.shape
    dff = w_down.shape[0]
    fc = FFN_CHUNK
    nch = dff // fc
    pair = lambda t: jnp.concatenate([t[..., :dff].reshape(t.shape[:-1] + (nch, fc)),
                                      t[..., dff:].reshape(t.shape[:-1] + (nch, fc))], axis=-1)
    cw = jnp.transpose(pair(w_dw), (1, 0, 2))
    cb = pair(b_dw).reshape(nch, 1, 2 * fc)
    wup = w_up.astype(BF16)
    wdn = w_down.astype(BF16)
    return pl.pallas_call(
        functools.partial(_ffn_kernel, tile=tile, d=d, n_chunks=nch),
        grid=(b, s // tile),
        in_specs=_halo_specs(tile, d, s) + [
            pl.BlockSpec((1, 1, mod.shape[2]), lambda bb, i: (bb, 0, 0)),
            _const_spec((1, d)), _const_spec((1, d)),
            _const_spec(wup.shape), _const_spec(cw.shape), _const_spec(cb.shape), _const_spec(wdn.shape)],
        out_specs=pl.BlockSpec((1, tile, d), lambda bb, i: (bb, i, 0)),
        out_shape=jax.ShapeDtypeStruct((b, s, d), F32),
        scratch_shapes=[pltpu.VMEM((tile + 2 * HALO, d), BF16),
                        pltpu.VMEM((2, tile + 2 * HALO, 2 * fc), F32),
                        pltpu.VMEM((tile, d), F32)],
        compiler_params=_cparams(2),
        name="convffn",
    )(h, h, h, mod, g_pre.reshape(1, d), g_post.reshape(1, d), wup, cw, cb, wdn)


CONF_TOKENS = 64


def _conformer_kernel(prev_ref, main_ref, next_ref, mod_ref, gpre_ref, gpost_ref, w1_ref, b1_ref,
                      dw_ref, db_ref, lg_ref, lb_ref, w2_ref, b2_ref, o_ref, hn_ref, u_ref, c_ref,
                      *, tile, d, seq):
    shift = mod_ref[0, :, 0:d]
    scale = mod_ref[0, :, d:2 * d]
    gate = mod_ref[0, :, 2 * d:3 * d]
    g = gpre_ref[...]
    _fill_halo_tile(hn_ref, prev_ref, main_ref, next_ref,
                    lambda t: _mod_norm(t, g, scale, shift), tile)
    ag = _dot(hn_ref[...], w1_ref[...]) + b1_ref[...]
    pos = (pl.program_id(1) * tile - HALO
           + lax.broadcasted_iota(jnp.int32, (tile + 2 * HALO, 1), 0))
    u = jnp.where((pos >= 0) & (pos < seq), ag[:, 0:d] * jax.nn.sigmoid(ag[:, d:2 * d]), 0.0)
    groups = d // LANES
    n_rows = tile + 2 * HALO
    for j in range(groups):
        u_ref[pl.ds(j, n_rows, stride=groups), :] = u[:, j * LANES:(j + 1) * LANES]
    first = HALO - CONF_K // 2

    def token_tile(ref, t):
        return ref[pl.ds(pl.multiple_of(t * groups, groups), groups), :]

    def conv_block(tb, carry):
        t0 = tb * CONF_TOKENS
        for i in range(CONF_TOKENS):
            acc = db_ref[...]
            for k in range(CONF_K):
                acc = acc + dw_ref[k] * token_tile(u_ref, t0 + i + first + k)
            c_ref[pl.ds(pl.multiple_of((t0 + i) * groups, groups), groups), :] = acc
        return carry
    lax.fori_loop(0, tile // CONF_TOKENS, conv_block, 0)
    c = jnp.concatenate([c_ref[pl.ds(j, tile, stride=groups), :] for j in range(groups)], axis=1)
    mu = jnp.mean(c, axis=-1, keepdims=True)
    cc = c - mu
    var = jnp.mean(cc * cc, axis=-1, keepdims=True)
    ln = cc * lax.rsqrt(var + EPS) * lg_ref[...] + lb_ref[...]
    y = _dot(_silu(ln).astype(BF16), w2_ref[...]) + b2_ref[...]
    o_ref[0] = main_ref[0] + gate * _rms(y, gpost_ref[...])


def _conformer(h, mod, g_pre, g_post, w_pw1, b_pw1, w_dw, b_dw, ln_g, ln_b, w_pw2, b_pw2, tile):
    b, s, d = h.shape
    groups = d // LANES
    assert groups == F32_SUBLANES and tile % CONF_TOKENS == 0
    row = lambda t: t.reshape(1, -1)
    return pl.pallas_call(
        functools.partial(_conformer_kernel, tile=tile, d=d, seq=s),
        grid=(b, s // tile),
        in_specs=_halo_specs(tile, d, s) + [
            pl.BlockSpec((1, 1, mod.shape[2]), lambda bb, i: (bb, 0, 0)),
            _const_spec((1, d)), _const_spec((1, d)),
            _const_spec(w_pw1.shape), _const_spec((1, 2 * d)),
            _const_spec((CONF_K, groups, LANES)), _const_spec((groups, LANES)),
            _const_spec((1, d)), _const_spec((1, d)),
            _const_spec(w_pw2.shape), _const_spec((1, d))],
        out_specs=pl.BlockSpec((1, tile, d), lambda bb, i: (bb, i, 0)),
        out_shape=jax.ShapeDtypeStruct((b, s, d), F32),
        scratch_shapes=[pltpu.VMEM((tile + 2 * HALO, d), BF16),
                        pltpu.VMEM(((tile + 2 * HALO) * groups, LANES), F32),
                        pltpu.VMEM((tile * groups, LANES), F32)],
        compiler_params=_cparams(2),
        name="conformer",
    )(h, h, h, mod, row(g_pre), row(g_post), w_pw1.astype(BF16), row(b_pw1),
      w_dw.reshape(CONF_K, groups, LANES), b_dw.reshape(groups, LANES),
      row(ln_g), row(ln_b), w_pw2.astype(BF16), row(b_pw2))


def _token_tile(seq):
    return min(seq, 1024)


def kernel(x, c, ctx, c_ctx, w_mod, b_mod, g_mix_pre, g_mix_post, g_ffn_pre, g_ffn_post, w_in, w_out, hy_short_w, hy_short_b, hy_f_w1, hy_f_b1, hy_f_w2, hy_f_b2, hy_f_w3, hy_f_b3, hy_f_w4, hy_f_freq, hy_bias, na_rpb, cf_w_pw1, cf_b_pw1, cf_w_dw, cf_b_dw, cf_ln_g, cf_ln_b, cf_w_pw2, cf_b_pw2, ffn_w_up, ffn_w_dw, ffn_b_dw, ffn_w_down):
    bsz, seq, d = x.shape
    depth = w_mod.shape[0]
    hw = hy_bias.shape[1]
    aw = NA_HEADS * NA_HEAD_DIM
    tile = _token_tile(seq)
    assert seq % tile == 0 and seq % (GRID_W * NA_WIN_ROWS) == 0 and tile % HALO == 0

    n_rows = -(-(bsz + 1) // BF16_SUBLANES) * BF16_SUBLANES
    cc = jnp.concatenate([c, c_ctx[None, :], jnp.zeros((n_rows - bsz - 1, d), F32)], axis=0)
    mod_all = _adaln(cc, w_mod, b_mod)

    h = x
    for layer in range(depth):
        mod = mod_all[layer, :bsz].reshape(bsz, 1, 6 * d)
        if layer % 2 == 0:
            e = layer // 2
            w_e = w_in[e].astype(BF16)
            k_ctx, v_ctx = _ctxkv(ctx, mod_all[layer, bsz:bsz + 1, :2 * d], g_mix_pre[layer],
                                  w_e[:, 3 * hw + aw:])
            x0, xv, q, k, v = _inproj(h, mod, g_mix_pre[layer], w_e[:, :3 * hw], w_e[:, 3 * hw:],
                                      hy_short_w[e], hy_short_b[e], tile)
            f1, m2f, m2i, f3 = _dft_tables(seq)
            hf, hb = _hyena_filters(seq, hy_f_w1[e], hy_f_b1[e], hy_f_w2[e], hy_f_b2[e],
                                    hy_f_w3[e], hy_f_b3[e], hy_f_w4[e], hy_f_freq[e])
            kf = _filter_spectrum(hf, hb, f1, m2f)
            y_hy = _longconv(x0, xv, kf, hy_bias[e], f1, m2f, m2i, f3)
            y_na = _nattn(q, k, v, k_ctx, v_ctx, _bias_table(na_rpb[e]))
            h = _outproj(y_hy, y_na, h, mod, g_mix_post[layer], w_out[e].astype(BF16), tile)
        else:
            o = layer // 2
            h = _conformer(h, mod, g_mix_pre[layer], g_mix_post[layer], cf_w_pw1[o], cf_b_pw1[o],
                           cf_w_dw[o], cf_b_dw[o], cf_ln_g[o], cf_ln_b[o], cf_w_pw2[o], cf_b_pw2[o], tile)
        h = _ffn(h, mod, g_ffn_pre[layer], g_ffn_post[layer], ffn_w_up[layer], ffn_w_dw[layer],
                 ffn_b_dw[layer], ffn_w_down[layer], tile)
    return h
```

```python
import functools
import math

import numpy as np
import jax
import jax.numpy as jnp
from jax import lax
from jax.experimental import pallas as pl
from jax.experimental.pallas import tpu as pltpu

F32 = jnp.float32
BF16 = jnp.bfloat16

EPS = 1e-6
NEG_BIG = -1e30
LOG2_E = math.log2(math.e)

GRID_W = 64
NA_HEADS = 8
NA_HEAD_DIM = 64
NA_WIN_ROWS = 8
NA_WIN_COLS = 16
HY_SHORT_K = 3
HY_EMB_DIM = 33
HY_DECAY_FAST = 0.3
HY_DECAY_SLOW = 1.5
HY_DECAY_TARGET = 1e-2
CONF_K = 31
FFN_CONV_K = 3

LANES = 128
F32_SUBLANES = 8
BF16_SUBLANES = 16
VMEM_LIMIT = 56 * 1024 * 1024

HALO = BF16_SUBLANES
DFT_N2 = 64
DFT_UNROLL = 16
NA_UNROLL = 4
FFN_CHUNK = 256
EMB_PAD = LANES


def _cparams(n_axes):
    return pltpu.CompilerParams(dimension_semantics=("arbitrary",) * n_axes,
                                vmem_limit_bytes=VMEM_LIMIT)


def _const_spec(shape):
    nd = len(shape)
    return pl.BlockSpec(shape, lambda *_: (0,) * nd, pipeline_mode=pl.Buffered(1))


def _silu(t):
    return t * jax.nn.sigmoid(t)


def _mod_norm(t, g, scale, shift):
    ms = jnp.mean(t * t, axis=-1, keepdims=True)
    return (t * lax.rsqrt(ms + EPS) * g) * (1.0 + scale) + shift


def _rms(t, g):
    ms = jnp.mean(t * t, axis=-1, keepdims=True)
    return t * lax.rsqrt(ms + EPS) * g


def _dot(a, b):
    return jnp.dot(a, b, preferred_element_type=F32)


def _dot_hi(a, b):
    return jnp.dot(a, b, preferred_element_type=F32, precision=lax.Precision.HIGHEST)


def _fill_halo_tile(hn_ref, prev_ref, main_ref, next_ref, norm_fn, tile):
    i = pl.program_id(1)
    last = pl.num_programs(1) - 1
    p = norm_fn(prev_ref[0])
    n = norm_fn(next_ref[0])
    hn_ref[0:HALO, :] = jnp.where(i > 0, p, 0.0).astype(BF16)
    hn_ref[HALO:HALO + tile, :] = norm_fn(main_ref[0]).astype(BF16)
    hn_ref[HALO + tile:, :] = jnp.where(i < last, n, 0.0).astype(BF16)


def _halo_specs(tile, d, seq):
    r = tile // HALO
    nblk = seq // HALO
    return [
        pl.BlockSpec((1, HALO, d), lambda b, i: (b, jnp.maximum(i * r - 1, 0), 0)),
        pl.BlockSpec((1, tile, d), lambda b, i: (b, i, 0)),
        pl.BlockSpec((1, HALO, d), lambda b, i: (b, jnp.minimum((i + 1) * r, nblk - 1), 0)),
    ]


def _adaln_kernel(c_ref, w_ref, b_ref, o_ref):
    s = _silu(c_ref[...]).astype(BF16)
    o_ref[0] = _dot(s, w_ref[0].astype(BF16)) + b_ref[0]


def _adaln(cc, w_mod, b_mod):
    depth, d, n6 = w_mod.shape
    r = cc.shape[0]
    nb = d
    return pl.pallas_call(
        _adaln_kernel,
        grid=(depth, n6 // nb),
        in_specs=[pl.BlockSpec((r, d), lambda l, j: (0, 0)),
                  pl.BlockSpec((1, d, nb), lambda l, j: (l, 0, j)),
                  pl.BlockSpec((1, 1, nb), lambda l, j: (l, 0, j))],
        out_specs=pl.BlockSpec((1, r, nb), lambda l, j: (l, 0, j)),
        out_shape=jax.ShapeDtypeStruct((depth, r, n6), F32),
        compiler_params=_cparams(2),
        name="adaln",
    )(cc, w_mod, b_mod.reshape(depth, 1, n6))


def _inproj_kernel(prev_ref, main_ref, next_ref, mod_ref, g_ref, why_ref, wqkv_ref, sw_ref, sb_ref,
                   x0_ref, xv_ref, q_ref, k_ref, v_ref, hn_ref, z_ref, *, tile, d, hw, aw):
    shift = mod_ref[0, :, 0:d]
    scale = mod_ref[0, :, d:2 * d]
    g = g_ref[...]
    _fill_halo_tile(hn_ref, prev_ref, main_ref, next_ref,
                    lambda t: _mod_norm(t, g, scale, shift), tile)
    z_ref[...] = _dot(hn_ref[...], why_ref[...])
    uc = (sw_ref[0:1, :] * z_ref[pl.ds(HALO - 1, tile), :]
          + sw_ref[1:2, :] * z_ref[pl.ds(HALO, tile), :]
          + sw_ref[2:3, :] * z_ref[pl.ds(HALO + 1, tile), :]
          + sb_ref[...])
    x0_ref[0] = uc[:, 0:hw]
    xv_ref[0] = uc[:, hw:2 * hw] * uc[:, 2 * hw:3 * hw]
    qkv = _dot(hn_ref[pl.ds(HALO, tile), :], wqkv_ref[...])
    q_ref[0] = (qkv[:, 0:aw] * (NA_HEAD_DIM ** -0.5 * LOG2_E)).astype(BF16)
    k_ref[0] = qkv[:, aw:2 * aw].astype(BF16)
    v_ref[0] = qkv[:, 2 * aw:3 * aw].astype(BF16)


def _inproj(h, mod, g, w_hy, w_qkv, short_w, short_b, tile):
    b, s, d = h.shape
    hw = w_hy.shape[1] // 3
    aw = w_qkv.shape[1] // 3
    tok = lambda width: pl.BlockSpec((1, tile, width), lambda bb, i: (bb, i, 0))
    return pl.pallas_call(
        functools.partial(_inproj_kernel, tile=tile, d=d, hw=hw, aw=aw),
        grid=(b, s // tile),
        in_specs=_halo_specs(tile, d, s) + [
            pl.BlockSpec((1, 1, mod.shape[2]), lambda bb, i: (bb, 0, 0)),
            _const_spec((1, d)),
            _const_spec(w_hy.shape),
            _const_spec(w_qkv.shape),
            _const_spec(short_w.shape),
            _const_spec((1, 3 * hw)),
        ],
        out_specs=[tok(hw), tok(hw), tok(aw), tok(aw), tok(aw)],
        out_shape=[jax.ShapeDtypeStruct((b, s, hw), F32), jax.ShapeDtypeStruct((b, s, hw), F32),
                   jax.ShapeDtypeStruct((b, s, aw), BF16), jax.ShapeDtypeStruct((b, s, aw), BF16),
                   jax.ShapeDtypeStruct((b, s, aw), BF16)],
        scratch_shapes=[pltpu.VMEM((tile + 2 * HALO, d), BF16),
                        pltpu.VMEM((tile + 2 * HALO, 3 * hw), F32)],
        compiler_params=_cparams(2),
        name="inproj",
    )(h, h, h, mod, g.reshape(1, d), w_hy, w_qkv, short_w, short_b.reshape(1, 3 * hw))


def _ctxkv_kernel(ctx_ref, mod_ref, g_ref, w_ref, k_ref, v_ref, *, d, aw):
    shift = mod_ref[:, 0:d]
    scale = mod_ref[:, d:2 * d]
    cn = _mod_norm(ctx_ref[0], g_ref[...], scale, shift).astype(BF16)
    kv = _dot(cn, w_ref[...])
    k_ref[0] = kv[:, 0:aw].astype(BF16)
    v_ref[0] = kv[:, aw:2 * aw].astype(BF16)


def _ctxkv(ctx, mod_ctx, g, w_kv):
    b, n, d = ctx.shape
    aw = w_kv.shape[1] // 2
    out = pl.BlockSpec((1, n, aw), lambda bb: (bb, 0, 0))
    return pl.pallas_call(
        functools.partial(_ctxkv_kernel, d=d, aw=aw),
        grid=(b,),
        in_specs=[pl.BlockSpec((1, n, d), lambda bb: (bb, 0, 0)),
                  _const_spec(mod_ctx.shape), _const_spec((1, d)), _const_spec(w_kv.shape)],
        out_specs=[out, out],
        out_shape=[jax.ShapeDtypeStruct((b, n, aw), BF16)] * 2,
        compiler_params=_cparams(1),
        name="ctxkv",
    )(ctx, mod_ctx, g.reshape(1, d), w_kv)


def _filter_kernel(z_ref, w1_ref, b1_ref, w2_ref, b2_ref, w3_ref, b3_ref, w4_ref, fr_ref, dl_ref,
                   hf_ref, hb_ref, *, rows, hw):
    half = rows // 2
    z = jnp.concatenate([z_ref[0:half, :], z_ref[half:rows, :]], axis=1)
    hdn = jnp.sin(fr_ref[0:1, :] * (_dot_hi(z, w1_ref[...]) + b1_ref[...]))
    hdn = jnp.sin(fr_ref[1:2, :] * (_dot_hi(hdn, w2_ref[...]) + b2_ref[...]))
    hdn = jnp.sin(fr_ref[2:3, :] * (_dot_hi(hdn, w3_ref[...]) + b3_ref[...]))
    filt = _dot_hi(hdn, w4_ref[...])
    for part in range(2):
        rs = slice(part * half, (part + 1) * half)
        f = filt[:, part * 2 * hw:(part + 1) * 2 * hw]
        decay = jnp.exp(-z_ref[rs, 0:1] * dl_ref[...])
        hf_ref[rs, :] = f[:, 0:hw] * decay
        pos = pl.program_id(0) * rows + part * half + lax.broadcasted_iota(jnp.int32, (half, 1), 0)
        hb_ref[rs, :] = jnp.where(pos > 0, f[:, hw:2 * hw] * decay, 0.0)


def _hyena_filters(length, w1, b1, w2, b2, w3, b3, w4, freq):
    hid = w1.shape[1]
    hw = w4.shape[1] // 2
    bands = (HY_EMB_DIM - 1) // 2
    t01 = np.linspace(0.0, 1.0, length)[:, None]
    ang = (2.0 * np.pi * np.arange(length) / length)[:, None] * np.linspace(1e-4, bands - 1, bands)[None, :]
    z = jnp.asarray(np.concatenate([t01, np.cos(ang), -np.sin(ang),
                                    np.zeros((length, EMB_PAD - HY_EMB_DIM))], axis=-1), F32)
    w1p = jnp.concatenate([w1, jnp.zeros((EMB_PAD - HY_EMB_DIM, hid), F32)], axis=0)
    deltas = jnp.asarray(np.abs(np.linspace(math.log(HY_DECAY_TARGET) / HY_DECAY_FAST,
                                            math.log(HY_DECAY_TARGET) / HY_DECAY_SLOW, hw))[None, :], F32)
    rows = min(length, 1024)
    out = pl.BlockSpec((rows, hw), lambda i: (i, 0))
    twice = lambda v: jnp.concatenate([v, v], axis=-1)
    diag2 = lambda w: jnp.concatenate([jnp.concatenate([w, jnp.zeros_like(w)], axis=1),
                                       jnp.concatenate([jnp.zeros_like(w), w], axis=1)], axis=0)
    args = (z, diag2(w1p), twice(b1.reshape(1, hid)), diag2(w2), twice(b2.reshape(1, hid)),
            diag2(w3), twice(b3.reshape(1, hid)), diag2(w4), twice(freq), deltas)
    return pl.pallas_call(
        functools.partial(_filter_kernel, rows=rows, hw=hw),
        grid=(length // rows,),
        in_specs=[pl.BlockSpec((rows, EMB_PAD), lambda i: (i, 0))] + [_const_spec(a.shape) for a in args[1:]],
        out_specs=[out, out],
        out_shape=[jax.ShapeDtypeStruct((length, hw), F32)] * 2,
        compiler_params=_cparams(1),
        name="hyena_filter",
    )(*args)


def _dft_tables(seq):
    n2 = DFT_N2
    h1 = seq // n2
    n1 = 2 * h1
    hk = n1 // 2
    n = n1 * n2
    k1 = np.arange(hk)[:, None]
    t1 = np.arange(h1)[None, :]
    a1 = 2.0 * np.pi * ((k1 * t1) % n1) / n1
    nyquist = np.where(t1 % 2 == 0, 1.0, -1.0)
    twice = np.where(k1 == 0, 1.0, 2.0)
    f1_im, f3_im = -np.sin(a1), -twice * np.sin(a1)
    f1_im[0:1], f3_im[0:1] = nyquist, nyquist
    f1 = np.concatenate([np.cos(a1), f1_im], axis=0)
    f3 = np.concatenate([(twice * np.cos(a1)).T, f3_im.T], axis=1) / n
    kk1 = np.arange(hk + 1)[:, None, None]
    k2 = np.arange(n2)[None, :, None]
    t2 = np.arange(n2)[None, None, :]
    th = 2.0 * np.pi * ((t2 * k2 * n1 + t2 * kk1) % n) / n
    wr, wi = np.cos(th), -np.sin(th)
    m2f = np.concatenate([np.concatenate([wr, -wi], axis=2),
                          np.concatenate([wi, wr], axis=2)], axis=1)
    m2i = np.transpose(m2f, (0, 2, 1))
    as_bf16 = lambda a: jnp.asarray(a, F32).astype(BF16)
    return as_bf16(f1), as_bf16(m2f), as_bf16(m2i), as_bf16(f3)


def _pitch(rows):
    return rows + F32_SUBLANES


def _dft_stage1(load_rows, f1_ref, a_ref, n1):
    def body(t2, carry):
        p = _dot(f1_ref[...], load_rows(t2).astype(BF16))
        a_ref[pl.ds(pl.multiple_of(t2 * _pitch(n1), F32_SUBLANES), n1), :] = p
        return carry
    lax.fori_loop(0, DFT_N2, body, 0, unroll=DFT_UNROLL)


def _dft_stage2(a_ref, m2f_ref, k1, n1):
    hk = n1 // 2
    if k1 is None:
        ar = a_ref[pl.ds(hk, DFT_N2, stride=_pitch(n1)), :]
        ai = jnp.zeros_like(ar)
        k1 = hk
    else:
        ar = a_ref[pl.ds(k1, DFT_N2, stride=_pitch(n1)), :]
        ai = jnp.where(k1 == 0, 0.0, a_ref[pl.ds(hk + k1, DFT_N2, stride=_pitch(n1)), :])
    return _dot(m2f_ref[k1], jnp.concatenate([ar, ai], axis=0).astype(BF16))


def _spectrum_kernel(hf_ref, hb_ref, f1_ref, m2f_ref, kf_ref, af_ref, ab_ref, *, n1):
    n2 = DFT_N2
    hk = n1 // 2
    _dft_stage1(lambda t2: hf_ref[pl.ds(t2, hk, stride=n2), :], f1_ref, af_ref, n1)
    _dft_stage1(lambda t2: hb_ref[pl.ds(t2, hk, stride=n2), :], f1_ref, ab_ref, n1)

    def spectrum(k1):
        xf = _dft_stage2(af_ref, m2f_ref, k1, n1)
        xb = _dft_stage2(ab_ref, m2f_ref, k1, n1)
        sign = jnp.where(lax.broadcasted_iota(jnp.int32, (2 * n2, 1), 0) < n2, 1.0, -1.0)
        return xf + sign * xb

    def body(k1, carry):
        kf_ref[k1] = spectrum(k1)
        return carry
    lax.fori_loop(0, hk, body, 0, unroll=DFT_UNROLL)
    kf_ref[hk] = spectrum(None)


def _filter_spectrum(hf, hb, f1, m2f):
    s, hw = hf.shape
    n2 = DFT_N2
    n1 = 2 * s // n2
    cb = LANES
    col = pl.BlockSpec((s, cb), lambda j: (0, j))
    return pl.pallas_call(
        functools.partial(_spectrum_kernel, n1=n1),
        grid=(hw // cb,),
        in_specs=[col, col, _const_spec(f1.shape), _const_spec(m2f.shape)],
        out_specs=pl.BlockSpec((n1 // 2 + 1, 2 * n2, cb), lambda j: (0, 0, j)),
        out_shape=jax.ShapeDtypeStruct((n1 // 2 + 1, 2 * n2, hw), F32),
        scratch_shapes=[pltpu.VMEM((n2 * _pitch(n1), cb), F32), pltpu.VMEM((n2 * _pitch(n1), cb), F32)],
        compiler_params=_cparams(1),
        name="filter_spectrum",
    )(hf, hb, f1, m2f)


def _longconv_kernel(x0_ref, xv_ref, kf_ref, bias_ref, f1_ref, m2f_ref, m2i_ref, f3_ref, o_ref,
                     xs_ref, a_ref, b_ref, y_ref, *, n1):
    n2 = DFT_N2
    h1 = hk = n1 // 2

    def pitch_rows(t1, carry):
        xs_ref[pl.ds(pl.multiple_of(t1 * _pitch(n2), F32_SUBLANES), n2), :] = (
            xv_ref[0, pl.ds(pl.multiple_of(t1 * n2, n2), n2), :])
        return carry
    lax.fori_loop(0, h1, pitch_rows, 0, unroll=DFT_UNROLL)
    _dft_stage1(lambda t2: xs_ref[pl.ds(t2, h1, stride=_pitch(n2)), :], f1_ref, a_ref, n1)

    def filtered(k1, row):
        x = _dft_stage2(a_ref, m2f_ref, k1, n1)
        kf = kf_ref[row]
        xr, xi = x[0:n2], x[n2:2 * n2]
        kr, ki = kf[0:n2], kf[n2:2 * n2]
        y = jnp.concatenate([xr * kr - xi * ki, xr * ki + xi * kr], axis=0).astype(BF16)
        return _dot(m2i_ref[row], y)

    def per_k1(k1, carry):
        b_ref[pl.ds(pl.multiple_of(k1 * _pitch(2 * n2), F32_SUBLANES), 2 * n2), :] = filtered(k1, k1)
        return carry
    lax.fori_loop(0, hk, per_k1, 0, unroll=DFT_UNROLL)
    b_ref[pl.ds(n2, n2), :] = filtered(None, hk)[0:n2]

    def per_t2(t2, carry):
        br = b_ref[pl.ds(t2, hk, stride=_pitch(2 * n2)), :]
        bi = b_ref[pl.ds(n2 + t2, hk, stride=_pitch(2 * n2)), :]
        y = _dot(f3_ref[...], jnp.concatenate([br, bi], axis=0).astype(BF16))
        y_ref[pl.ds(pl.multiple_of(t2 * _pitch(h1), F32_SUBLANES), h1), :] = y
        return carry
    lax.fori_loop(0, n2, per_t2, 0, unroll=DFT_UNROLL)

    def per_t1(t1, carry):
        rows = pl.ds(pl.multiple_of(t1 * n2, n2), n2)
        y = y_ref[pl.ds(t1, n2, stride=_pitch(h1)), :]
        o_ref[0, rows, :] = (x0_ref[0, rows, :] * (y + xv_ref[0, rows, :] * bias_ref[...])).astype(BF16)
        return carry
    lax.fori_loop(0, h1, per_t1, 0, unroll=DFT_UNROLL)


def _longconv(x0, xv, kf, bias, f1, m2f, m2i, f3):
    b, s, hw = x0.shape
    n2 = DFT_N2
    n1 = 2 * s // n2
    cb = LANES
    tok = pl.BlockSpec((1, s, cb), lambda j, bb: (bb, 0, j))
    return pl.pallas_call(
        functools.partial(_longconv_kernel, n1=n1),
        grid=(hw // cb, b),
        in_specs=[tok, tok,
                  pl.BlockSpec((n1 // 2 + 1, 2 * n2, cb), lambda j, bb: (0, 0, j), pipeline_mode=pl.Buffered(1)),
                  pl.BlockSpec((1, cb), lambda j, bb: (0, j)),
                  _const_spec(f1.shape), _const_spec(m2f.shape), _const_spec(m2i.shape),
                  _const_spec(f3.shape)],
        out_specs=tok,
        out_shape=jax.ShapeDtypeStruct((b, s, hw), BF16),
        scratch_shapes=[pltpu.VMEM((n1 // 2 * _pitch(n2), cb), F32), pltpu.VMEM((n2 * _pitch(n1), cb), F32),
                        pltpu.VMEM((n1 // 2 * _pitch(2 * n2), cb), F32), pltpu.VMEM((n2 * _pitch(n1 // 2), cb), F32)],
        compiler_params=_cparams(2),
        name="longconv",
    )(x0, xv, kf, bias.reshape(1, hw), f1, m2f, m2i, f3)


NA_QROWS = NA_WIN_ROWS // 2
NA_BAND_ROWS = NA_QROWS + NA_WIN_ROWS


def _bias_kernel(rpb_ref, o_ref, *, n_dc):
    h = pl.program_id(0)
    n_dr = 2 * NA_WIN_ROWS - 1
    qcol = lax.broadcasted_iota(jnp.int32, (GRID_W, LANES), 0)
    lane = lax.broadcasted_iota(jnp.int32, (GRID_W, LANES), 1)
    kcol = lane % GRID_W
    win = jnp.clip(qcol - NA_WIN_COLS // 2, 0, GRID_W - NA_WIN_COLS)
    col_ok = (kcol >= win) & (kcol < win + NA_WIN_COLS)
    low_half = lane < GRID_W
    lane_row = lax.broadcasted_iota(jnp.int32, (1, LANES), 1)
    toeplitz = []
    for dr in range(n_dr):
        vec = jnp.zeros((1, LANES), F32)
        for d in range(n_dc):
            vec = jnp.where(lane_row == d, rpb_ref[(h * n_dr + dr) * n_dc + d], vec)
        rows = jnp.broadcast_to(vec * LOG2_E, (GRID_W, LANES))
        toeplitz.append([pltpu.roll(rows, (LANES - (NA_WIN_COLS - 1) + half * GRID_W) % LANES, 1,
                                    stride=1, stride_axis=0) for half in range(2)])
    neg = jnp.full((GRID_W, LANES), NEG_BIG, F32)
    for pos in range(3):
        for p in range(NA_QROWS):
            lo = (0, p, NA_QROWS)[pos]
            for jj in range(NA_BAND_ROWS // 2):
                halves = []
                for half in range(2):
                    j = 2 * jj + half
                    dr = j - p + NA_WIN_ROWS - 1 - NA_QROWS * pos
                    halves.append(toeplitz[dr][half] if lo <= j < lo + NA_WIN_ROWS else neg)
                tile = jnp.where(low_half, halves[0], halves[1])
                o_ref[pos, 0, p * GRID_W:(p + 1) * GRID_W, jj * LANES:(jj + 1) * LANES] = (
                    jnp.where(col_ok, tile, NEG_BIG))


def _bias_table(rpb):
    heads, n_dr, n_dc = rpb.shape
    blk = (NA_QROWS * GRID_W, NA_BAND_ROWS * GRID_W)
    assert 2 * GRID_W == LANES and NA_BAND_ROWS % 2 == 0 and n_dc <= LANES
    return pl.pallas_call(
        functools.partial(_bias_kernel, n_dc=n_dc),
        grid=(heads,),
        in_specs=[pl.BlockSpec(memory_space=pltpu.SMEM)],
        out_specs=pl.BlockSpec((3, 1) + blk, lambda h: (0, h, 0, 0)),
        out_shape=jax.ShapeDtypeStruct((3, heads) + blk, F32),
        compiler_params=_cparams(1),
        name="na_bias",
    )(rpb.reshape(-1))


def _nattn_kernel(q_ref, k_ref, v_ref, kc_ref, vc_ref, bias_ref, o_ref, vs_ref, vcs_ref, *, rows):
    dh = NA_HEAD_DIM
    nq = NA_QROWS * GRID_W
    band = NA_BAND_ROWS * GRID_W
    nt = (((1,), (1,)), ((), ()))

    def own_lanes(n_rows):
        lane = lax.broadcasted_iota(jnp.int32, (n_rows, LANES), 1)
        return [lane < dh, lane >= dh]

    for hh in range(2):
        vs_ref[hh] = jnp.where(own_lanes(v_ref.shape[1])[hh], v_ref[0], 1.0).astype(BF16)
        vcs_ref[hh] = jnp.where(own_lanes(vc_ref.shape[1])[hh], vc_ref[0], 1.0).astype(BF16)
    own_q = own_lanes(nq)
    n_blocks = rows // NA_QROWS

    def per_block(i, carry):
        r0 = i * NA_QROWS
        pos = jnp.where(i == 0, 0, jnp.where(i == n_blocks - 1, 2, 1))
        key0 = pl.multiple_of(jnp.clip(r0 - NA_WIN_ROWS // 2, 0, rows - NA_BAND_ROWS) * GRID_W, GRID_W)
        qrows = pl.ds(pl.multiple_of(r0 * GRID_W, nq), nq)
        q = q_ref[0, qrows, :]
        kb = k_ref[0, pl.ds(key0, band), :]
        outs = []
        for hh in range(2):
            qh = jnp.where(own_q[hh], q, 0.0).astype(BF16)
            s_loc = lax.dot_general(qh, kb, nt, preferred_element_type=F32) + bias_ref[pos, hh]
            s_ctx = lax.dot_general(qh, kc_ref[0], nt, preferred_element_type=F32)
            s = jnp.concatenate([s_loc, s_ctx], axis=-1)
            p = jnp.exp2(s - jnp.max(s, axis=-1, keepdims=True)).astype(BF16)
            o = _dot(p[:, 0:band], vs_ref[hh, pl.ds(key0, band), :]) + _dot(p[:, band:], vcs_ref[hh])
            outs.append(o / pltpu.roll(o, dh, axis=1))
        o_ref[0, qrows, :] = jnp.where(own_q[0], outs[0], outs[1]).astype(BF16)
        return carry
    lax.fori_loop(0, n_blocks, per_block, 0, unroll=NA_UNROLL)


def _nattn(q, k, v, kc, vc, bias):
    b, s, aw = q.shape
    nctx = kc.shape[1]
    hps = LANES // NA_HEAD_DIM
    assert (s // GRID_W) % NA_QROWS == 0 and s // GRID_W >= NA_BAND_ROWS
    tok = pl.BlockSpec((1, s, LANES), lambda j, bb: (bb, 0, j))
    ctx = pl.BlockSpec((1, nctx, LANES), lambda j, bb: (bb, 0, j))
    return pl.pallas_call(
        functools.partial(_nattn_kernel, rows=s // GRID_W),
        grid=(aw // LANES, b),
        in_specs=[tok, tok, tok, ctx, ctx,
                  pl.BlockSpec((3, hps) + bias.shape[2:], lambda j, bb: (0, j, 0, 0))],
        out_specs=tok,
        out_shape=jax.ShapeDtypeStruct((b, s, aw), BF16),
        scratch_shapes=[pltpu.VMEM((hps, s, LANES), BF16), pltpu.VMEM((hps, nctx, LANES), BF16)],
        compiler_params=_cparams(2),
        name="nattn",
    )(q, k, v, kc, vc, bias)


def _outproj_kernel(yh_ref, ya_ref, h_ref, mod_ref, g_ref, w_ref, o_ref, *, d, hw):
    gate = mod_ref[0, :, 2 * d:3 * d]
    y = _dot(yh_ref[0], w_ref[0:hw, :]) + _dot(ya_ref[0], w_ref[hw:, :])
    o_ref[0] = h_ref[0] + gate * _rms(y, g_ref[...])


def _outproj(y_hy, y_na, h, mod, g, w_out, tile):
    b, s, d = h.shape
    hw = y_hy.shape[2]
    aw = y_na.shape[2]
    tok = lambda width: pl.BlockSpec((1, tile, width), lambda bb, i: (bb, i, 0))
    return pl.pallas_call(
        functools.partial(_outproj_kernel, d=d, hw=hw),
        grid=(b, s // tile),
        in_specs=[tok(hw), tok(aw), tok(d),
                  pl.BlockSpec((1, 1, mod.shape[2]), lambda bb, i: (bb, 0, 0)),
                  _const_spec((1, d)), _const_spec(w_out.shape)],
        out_specs=tok(d),
        out_shape=jax.ShapeDtypeStruct((b, s, d), F32),
        compiler_params=_cparams(2),
        name="outproj",
    )(y_hy, y_na, h, mod, g.reshape(1, d), w_out)


def _ffn_kernel(prev_ref, main_ref, next_ref, mod_ref, gpre_ref, gpost_ref, wup_ref, cw_ref, cb_ref, wdn_ref,
                o_ref, hn_ref, z_ref, acc_ref, *, tile, d, n_chunks):
    fc = FFN_CHUNK
    shift = mod_ref[0, :, 3 * d:4 * d]
    scale = mod_ref[0, :, 4 * d:5 * d]
    gate = mod_ref[0, :, 5 * d:6 * d]
    g = gpre_ref[...]
    _fill_halo_tile(hn_ref, prev_ref, main_ref, next_ref,
                    lambda t: _mod_norm(t, g, scale, shift), tile)
    acc_ref[...] = jnp.zeros_like(acc_ref)

    def chunk_start(j, base):
        start = base + j * fc
        return start if isinstance(j, int) else pl.multiple_of(start, LANES)

    def up(j, slot):
        z_ref[slot, :, 0:fc] = _dot(hn_ref[...], wup_ref[:, pl.ds(chunk_start(j, 0), fc)])
        z_ref[slot, :, fc:2 * fc] = _dot(hn_ref[...], wup_ref[:, pl.ds(chunk_start(j, n_chunks * fc), fc)])

    def down(j, slot):
        cw = cw_ref[j]
        c = (cw[0:1, :] * z_ref[slot, pl.ds(HALO - 1, tile), :]
             + cw[1:2, :] * z_ref[slot, pl.ds(HALO, tile), :]
             + cw[2:3, :] * z_ref[slot, pl.ds(HALO + 1, tile), :]
             + cb_ref[j])
        a = (_silu(c[:, 0:fc]) * c[:, fc:2 * fc]).astype(BF16)
        acc_ref[...] += _dot(a, wdn_ref[pl.ds(chunk_start(j, 0), fc), :])

    def step(j, slot):
        up(j + 1, 1 - slot)
        down(j, slot)

    up(0, 0)
    n_pairs = (n_chunks - 1) // 2

    def pair(jj, carry):
        step(2 * jj, 0)
        step(2 * jj + 1, 1)
        return carry
    lax.fori_loop(0, n_pairs, pair, 0)
    if (n_chunks - 1) % 2:
        step(n_chunks - 2, 0)
    down(n_chunks - 1, (n_chunks - 1) % 2)
    o_ref[0] = main_ref[0] + gate * _rms(acc_ref[...], gpost_ref[...])


def _layer_spec(stacked, layer):
    nd = stacked.ndim - 1
    return pl.BlockSpec((None,) + stacked.shape[1:], lambda *_: (layer,) + (0,) * nd, pipeline_mode=pl.Buffered(1))


def _ffn(h, mod, g_pre, g_post, wup_all, w_dw, b_dw, wdn_all, layer, tile):
    b, s, d = h.shape
    dff = wdn_all.shape[1]
    fc = FFN_CHUNK
    nch = dff // fc
    pair = lambda t: jnp.concatenate([t[..., :dff].reshape(t.shape[:-1] + (nch, fc)),
                                      t[..., dff:].reshape(t.shape[:-1] + (nch, fc))], axis=-1)
    cw = jnp.transpose(pair(w_dw), (1, 0, 2))
    cb = pair(b_dw).reshape(nch, 1, 2 * fc)
    return pl.pallas_call(
        functools.partial(_ffn_kernel, tile=tile, d=d, n_chunks=nch),
        grid=(b, s // tile),
        in_specs=_halo_specs(tile, d, s) + [
            pl.BlockSpec((1, 1, mod.shape[2]), lambda bb, i: (bb, 0, 0)),
            _const_spec((1, d)), _const_spec((1, d)),
            _layer_spec(wup_all, layer), _const_spec(cw.shape), _const_spec(cb.shape), _layer_spec(wdn_all, layer)],
        out_specs=pl.BlockSpec((1, tile, d), lambda bb, i: (bb, i, 0)),
        out_shape=jax.ShapeDtypeStruct((b, s, d), F32),
        scratch_shapes=[pltpu.VMEM((tile + 2 * HALO, d), BF16),
                        pltpu.VMEM((2, tile + 2 * HALO, 2 * fc), F32),
                        pltpu.VMEM((tile, d), F32)],
        compiler_params=_cparams(2),
        name="convffn",
    )(h, h, h, mod, g_pre.reshape(1, d), g_post.reshape(1, d), wup_all, cw, cb, wdn_all)


CONF_TOKENS = 64


def _conformer_kernel(prev_ref, main_ref, next_ref, mod_ref, gpre_ref, gpost_ref, w1_ref, b1_ref,
                      dw_ref, db_ref, lg_ref, lb_ref, w2_ref, b2_ref, o_ref, hn_ref, u_ref, c_ref,
                      *, tile, d, seq):
    shift = mod_ref[0, :, 0:d]
    scale = mod_ref[0, :, d:2 * d]
    gate = mod_ref[0, :, 2 * d:3 * d]
    g = gpre_ref[...]
    _fill_halo_tile(hn_ref, prev_ref, main_ref, next_ref,
                    lambda t: _mod_norm(t, g, scale, shift), tile)
    ag = _dot(hn_ref[...], w1_ref[...]) + b1_ref[...]
    pos = (pl.program_id(1) * tile - HALO
           + lax.broadcasted_iota(jnp.int32, (tile + 2 * HALO, 1), 0))
    u = jnp.where((pos >= 0) & (pos < seq), ag[:, 0:d] * jax.nn.sigmoid(ag[:, d:2 * d]), 0.0)
    groups = d // LANES
    n_rows = tile + 2 * HALO
    for j in range(groups):
        u_ref[pl.ds(j, n_rows, stride=groups), :] = u[:, j * LANES:(j + 1) * LANES]
    first = HALO - CONF_K // 2

    def token_tile(ref, t):
        return ref[pl.ds(pl.multiple_of(t * groups, groups), groups), :]

    def conv_block(tb, carry):
        t0 = tb * CONF_TOKENS
        for i in range(CONF_TOKENS):
            acc = db_ref[...]
            for k in range(CONF_K):
                acc = acc + dw_ref[k] * token_tile(u_ref, t0 + i + first + k)
            c_ref[pl.ds(pl.multiple_of((t0 + i) * groups, groups), groups), :] = acc
        return carry
    lax.fori_loop(0, tile // CONF_TOKENS, conv_block, 0)
    c = jnp.concatenate([c_ref[pl.ds(j, tile, stride=groups), :] for j in range(groups)], axis=1)
    mu = jnp.mean(c, axis=-1, keepdims=True)
    cc = c - mu
    var = jnp.mean(cc * cc, axis=-1, keepdims=True)
    ln = cc * lax.rsqrt(var + EPS) * lg_ref[...] + lb_ref[...]
    y = _dot(_silu(ln).astype(BF16), w2_ref[...]) + b2_ref[...]
    o_ref[0] = main_ref[0] + gate * _rms(y, gpost_ref[...])


def _conformer(h, mod, g_pre, g_post, w_pw1, b_pw1, w_dw, b_dw, ln_g, ln_b, w_pw2, b_pw2, tile):
    b, s, d = h.shape
    groups = d // LANES
    assert groups == F32_SUBLANES and tile % CONF_TOKENS == 0
    row = lambda t: t.reshape(1, -1)
    return pl.pallas_call(
        functools.partial(_conformer_kernel, tile=tile, d=d, seq=s),
        grid=(b, s // tile),
        in_specs=_halo_specs(tile, d, s) + [
            pl.BlockSpec((1, 1, mod.shape[2]), lambda bb, i: (bb, 0, 0)),
            _const_spec((1, d)), _const_spec((1, d)),
            _const_spec(w_pw1.shape), _const_spec((1, 2 * d)),
            _const_spec((CONF_K, groups, LANES)), _const_spec((groups, LANES)),
            _const_spec((1, d)), _const_spec((1, d)),
            _const_spec(w_pw2.shape), _const_spec((1, d))],
        out_specs=pl.BlockSpec((1, tile, d), lambda bb, i: (bb, i, 0)),
        out_shape=jax.ShapeDtypeStruct((b, s, d), F32),
        scratch_shapes=[pltpu.VMEM((tile + 2 * HALO, d), BF16),
                        pltpu.VMEM(((tile + 2 * HALO) * groups, LANES), F32),
                        pltpu.VMEM((tile * groups, LANES), F32)],
        compiler_params=_cparams(2),
        name="conformer",
    )(h, h, h, mod, row(g_pre), row(g_post), w_pw1.astype(BF16), row(b_pw1),
      w_dw.reshape(CONF_K, groups, LANES), b_dw.reshape(groups, LANES),
      row(ln_g), row(ln_b), w_pw2.astype(BF16), row(b_pw2))


def _token_tile(seq):
    return min(seq, 1024)


def kernel(x, c, ctx, c_ctx, w_mod, b_mod, g_mix_pre, g_mix_post, g_ffn_pre, g_ffn_post, w_in, w_out, hy_short_w, hy_short_b, hy_f_w1, hy_f_b1, hy_f_w2, hy_f_b2, hy_f_w3, hy_f_b3, hy_f_w4, hy_f_freq, hy_bias, na_rpb, cf_w_pw1, cf_b_pw1, cf_w_dw, cf_b_dw, cf_ln_g, cf_ln_b, cf_w_pw2, cf_b_pw2, ffn_w_up, ffn_w_dw, ffn_b_dw, ffn_w_down):
    bsz, seq, d = x.shape
    depth = w_mod.shape[0]
    hw = hy_bias.shape[1]
    aw = NA_HEADS * NA_HEAD_DIM
    tile = _token_tile(seq)
    assert seq % tile == 0 and seq % (GRID_W * NA_WIN_ROWS) == 0 and tile % HALO == 0

    n_rows = -(-(bsz + 1) // BF16_SUBLANES) * BF16_SUBLANES
    cc = jnp.concatenate([c, c_ctx[None, :], jnp.zeros((n_rows - bsz - 1, d), F32)], axis=0)
    mod_all = _adaln(cc, w_mod, b_mod)
    ffn_wup, ffn_wdn = ffn_w_up.astype(BF16), ffn_w_down.astype(BF16)

    h = x
    for layer in range(depth):
        mod = mod_all[layer, :bsz].reshape(bsz, 1, 6 * d)
        if layer % 2 == 0:
            e = layer // 2
            w_e = w_in[e].astype(BF16)
            k_ctx, v_ctx = _ctxkv(ctx, mod_all[layer, bsz:bsz + 1, :2 * d], g_mix_pre[layer],
                                  w_e[:, 3 * hw + aw:])
            x0, xv, q, k, v = _inproj(h, mod, g_mix_pre[layer], w_e[:, :3 * hw], w_e[:, 3 * hw:],
                                      hy_short_w[e], hy_short_b[e], tile)
            f1, m2f, m2i, f3 = _dft_tables(seq)
            hf, hb = _hyena_filters(seq, hy_f_w1[e], hy_f_b1[e], hy_f_w2[e], hy_f_b2[e],
                                    hy_f_w3[e], hy_f_b3[e], hy_f_w4[e], hy_f_freq[e])
            kf = _filter_spectrum(hf, hb, f1, m2f)
            y_hy = _longconv(x0, xv, kf, hy_bias[e], f1, m2f, m2i, f3)
            y_na = _nattn(q, k, v, k_ctx, v_ctx, _bias_table(na_rpb[e]))
            h = _outproj(y_hy, y_na, h, mod, g_mix_post[layer], w_out[e].astype(BF16), tile)
        else:
            o = layer // 2
            h = _conformer(h, mod, g_mix_pre[layer], g_mix_post[layer], cf_w_pw1[o], cf_b_pw1[o],
                           cf_w_dw[o], cf_b_dw[o], cf_ln_g[o], cf_ln_b[o], cf_w_pw2[o], cf_b_pw2[o], tile)
        h = _ffn(h, mod, g_ffn_pre[layer], g_ffn_post[layer], ffn_wup, ffn_w_dw[layer],
                 ffn_b_dw[layer], ffn_wdn, layer, tile)
    return h
```

```python
import functools
import math

import numpy as np
import jax
import jax.numpy as jnp
from jax import lax
from jax.experimental import pallas as pl
from jax.experimental.pallas import tpu as pltpu

F32 = jnp.float32
BF16 = jnp.bfloat16

EPS = 1e-6
NEG_BIG = -1e30
LOG2_E = math.log2(math.e)

GRID_W = 64
NA_HEADS = 8
NA_HEAD_DIM = 64
NA_WIN_ROWS = 8
NA_WIN_COLS = 16
HY_SHORT_K = 3
HY_EMB_DIM = 33
HY_DECAY_FAST = 0.3
HY_DECAY_SLOW = 1.5
HY_DECAY_TARGET = 1e-2
CONF_K = 31
FFN_CONV_K = 3

LANES = 128
F32_SUBLANES = 8
BF16_SUBLANES = 16
VMEM_LIMIT = 56 * 1024 * 1024

HALO = BF16_SUBLANES
DFT_N2 = 64
DFT_UNROLL = True
NA_UNROLL = True
FFN_CHUNK = 256
EMB_PAD = LANES


def _cparams(n_axes):
    return pltpu.CompilerParams(dimension_semantics=("arbitrary",) * n_axes,
                                vmem_limit_bytes=VMEM_LIMIT)


def _const_spec(shape):
    nd = len(shape)
    return pl.BlockSpec(shape, lambda *_: (0,) * nd, pipeline_mode=pl.Buffered(1))


def _silu(t):
    return t * jax.nn.sigmoid(t)


def _mod_norm(t, g, scale, shift):
    ms = jnp.mean(t * t, axis=-1, keepdims=True)
    return (t * lax.rsqrt(ms + EPS) * g) * (1.0 + scale) + shift


def _rms(t, g):
    ms = jnp.mean(t * t, axis=-1, keepdims=True)
    return t * lax.rsqrt(ms + EPS) * g


def _dot(a, b):
    return jnp.dot(a, b, preferred_element_type=F32)


def _dot_hi(a, b):
    return jnp.dot(a, b, preferred_element_type=F32, precision=lax.Precision.HIGHEST)


def _fill_halo_tile(hn_ref, prev_ref, main_ref, next_ref, norm_fn, tile):
    i = pl.program_id(1)
    last = pl.num_programs(1) - 1
    p = norm_fn(prev_ref[0])
    n = norm_fn(next_ref[0])
    hn_ref[0:HALO, :] = jnp.where(i > 0, p, 0.0).astype(BF16)
    hn_ref[HALO:HALO + tile, :] = norm_fn(main_ref[0]).astype(BF16)
    hn_ref[HALO + tile:, :] = jnp.where(i < last, n, 0.0).astype(BF16)


def _halo_specs(tile, d, seq):
    r = tile // HALO
    nblk = seq // HALO
    return [
        pl.BlockSpec((1, HALO, d), lambda b, i: (b, jnp.maximum(i * r - 1, 0), 0)),
        pl.BlockSpec((1, tile, d), lambda b, i: (b, i, 0)),
        pl.BlockSpec((1, HALO, d), lambda b, i: (b, jnp.minimum((i + 1) * r, nblk - 1), 0)),
    ]


def _adaln_kernel(c_ref, w_ref, b_ref, o_ref):
    s = _silu(c_ref[...]).astype(BF16)
    o_ref[0] = _dot(s, w_ref[0].astype(BF16)) + b_ref[0]


def _adaln(cc, w_mod, b_mod):
    depth, d, n6 = w_mod.shape
    r = cc.shape[0]
    nb = d
    return pl.pallas_call(
        _adaln_kernel,
        grid=(depth, n6 // nb),
        in_specs=[pl.BlockSpec((r, d), lambda l, j: (0, 0)),
                  pl.BlockSpec((1, d, nb), lambda l, j: (l, 0, j)),
                  pl.BlockSpec((1, 1, nb), lambda l, j: (l, 0, j))],
        out_specs=pl.BlockSpec((1, r, nb), lambda l, j: (l, 0, j)),
        out_shape=jax.ShapeDtypeStruct((depth, r, n6), F32),
        compiler_params=_cparams(2),
        name="adaln",
    )(cc, w_mod, b_mod.reshape(depth, 1, n6))


def _inproj_kernel(prev_ref, main_ref, next_ref, mod_ref, g_ref, why_ref, wqkv_ref, sw_ref, sb_ref,
                   x0_ref, xv_ref, q_ref, k_ref, v_ref, hn_ref, z_ref, *, tile, d, hw, aw):
    shift = mod_ref[0, :, 0:d]
    scale = mod_ref[0, :, d:2 * d]
    g = g_ref[...]
    _fill_halo_tile(hn_ref, prev_ref, main_ref, next_ref,
                    lambda t: _mod_norm(t, g, scale, shift), tile)
    z_ref[...] = _dot(hn_ref[...], why_ref[...])
    uc = (sw_ref[0:1, :] * z_ref[pl.ds(HALO - 1, tile), :]
          + sw_ref[1:2, :] * z_ref[pl.ds(HALO, tile), :]
          + sw_ref[2:3, :] * z_ref[pl.ds(HALO + 1, tile), :]
          + sb_ref[...])
    x0_ref[0] = uc[:, 0:hw]
    xv_ref[0] = uc[:, hw:2 * hw] * uc[:, 2 * hw:3 * hw]
    qkv = _dot(hn_ref[pl.ds(HALO, tile), :], wqkv_ref[...])
    q_ref[0] = (qkv[:, 0:aw] * (NA_HEAD_DIM ** -0.5 * LOG2_E)).astype(BF16)
    k_ref[0] = qkv[:, aw:2 * aw].astype(BF16)
    v_ref[0] = qkv[:, 2 * aw:3 * aw].astype(BF16)


def _inproj(h, mod, g, w_hy, w_qkv, short_w, short_b, tile):
    b, s, d = h.shape
    hw = w_hy.shape[1] // 3
    aw = w_qkv.shape[1] // 3
    tok = lambda width: pl.BlockSpec((1, tile, width), lambda bb, i: (bb, i, 0))
    return pl.pallas_call(
        functools.partial(_inproj_kernel, tile=tile, d=d, hw=hw, aw=aw),
        grid=(b, s // tile),
        in_specs=_halo_specs(tile, d, s) + [
            pl.BlockSpec((1, 1, mod.shape[2]), lambda bb, i: (bb, 0, 0)),
            _const_spec((1, d)),
            _const_spec(w_hy.shape),
            _const_spec(w_qkv.shape),
            _const_spec(short_w.shape),
            _const_spec((1, 3 * hw)),
        ],
        out_specs=[tok(hw), tok(hw), tok(aw), tok(aw), tok(aw)],
        out_shape=[jax.ShapeDtypeStruct((b, s, hw), F32), jax.ShapeDtypeStruct((b, s, hw), F32),
                   jax.ShapeDtypeStruct((b, s, aw), BF16), jax.ShapeDtypeStruct((b, s, aw), BF16),
                   jax.ShapeDtypeStruct((b, s, aw), BF16)],
        scratch_shapes=[pltpu.VMEM((tile + 2 * HALO, d), BF16),
                        pltpu.VMEM((tile + 2 * HALO, 3 * hw), F32)],
        compiler_params=_cparams(2),
        name="inproj",
    )(h, h, h, mod, g.reshape(1, d), w_hy, w_qkv, short_w, short_b.reshape(1, 3 * hw))


def _ctxkv_kernel(ctx_ref, mod_ref, g_ref, w_ref, k_ref, v_ref, *, d, aw):
    shift = mod_ref[:, 0:d]
    scale = mod_ref[:, d:2 * d]
    cn = _mod_norm(ctx_ref[0], g_ref[...], scale, shift).astype(BF16)
    kv = _dot(cn, w_ref[...])
    k_ref[0] = kv[:, 0:aw].astype(BF16)
    v_ref[0] = kv[:, aw:2 * aw].astype(BF16)


def _ctxkv(ctx, mod_ctx, g, w_kv):
    b, n, d = ctx.shape
    aw = w_kv.shape[1] // 2
    out = pl.BlockSpec((1, n, aw), lambda bb: (bb, 0, 0))
    return pl.pallas_call(
        functools.partial(_ctxkv_kernel, d=d, aw=aw),
        grid=(b,),
        in_specs=[pl.BlockSpec((1, n, d), lambda bb: (bb, 0, 0)),
                  _const_spec(mod_ctx.shape), _const_spec((1, d)), _const_spec(w_kv.shape)],
        out_specs=[out, out],
        out_shape=[jax.ShapeDtypeStruct((b, n, aw), BF16)] * 2,
        compiler_params=_cparams(1),
        name="ctxkv",
    )(ctx, mod_ctx, g.reshape(1, d), w_kv)


def _filter_kernel(z_ref, w1_ref, b1_ref, w2_ref, b2_ref, w3_ref, b3_ref, w4_ref, fr_ref, dl_ref,
                   hf_ref, hb_ref, *, rows, hw):
    half = rows // 2
    z = jnp.concatenate([z_ref[0:half, :], z_ref[half:rows, :]], axis=1)
    hdn = jnp.sin(fr_ref[0:1, :] * (_dot_hi(z, w1_ref[...]) + b1_ref[...]))
    hdn = jnp.sin(fr_ref[1:2, :] * (_dot_hi(hdn, w2_ref[...]) + b2_ref[...]))
    hdn = jnp.sin(fr_ref[2:3, :] * (_dot_hi(hdn, w3_ref[...]) + b3_ref[...]))
    filt = _dot_hi(hdn, w4_ref[...])
    for part in range(2):
        rs = slice(part * half, (part + 1) * half)
        f = filt[:, part * 2 * hw:(part + 1) * 2 * hw]
        decay = jnp.exp(-z_ref[rs, 0:1] * dl_ref[...])
        hf_ref[rs, :] = f[:, 0:hw] * decay
        pos = pl.program_id(0) * rows + part * half + lax.broadcasted_iota(jnp.int32, (half, 1), 0)
        hb_ref[rs, :] = jnp.where(pos > 0, f[:, hw:2 * hw] * decay, 0.0)


def _hyena_filters(length, w1, b1, w2, b2, w3, b3, w4, freq):
    hid = w1.shape[1]
    hw = w4.shape[1] // 2
    bands = (HY_EMB_DIM - 1) // 2
    t01 = np.linspace(0.0, 1.0, length)[:, None]
    ang = (2.0 * np.pi * np.arange(length) / length)[:, None] * np.linspace(1e-4, bands - 1, bands)[None, :]
    z = jnp.asarray(np.concatenate([t01, np.cos(ang), -np.sin(ang),
                                    np.zeros((length, EMB_PAD - HY_EMB_DIM))], axis=-1), F32)
    w1p = jnp.concatenate([w1, jnp.zeros((EMB_PAD - HY_EMB_DIM, hid), F32)], axis=0)
    deltas = jnp.asarray(np.abs(np.linspace(math.log(HY_DECAY_TARGET) / HY_DECAY_FAST,
                                            math.log(HY_DECAY_TARGET) / HY_DECAY_SLOW, hw))[None, :], F32)
    rows = min(length, 1024)
    out = pl.BlockSpec((rows, hw), lambda i: (i, 0))
    twice = lambda v: jnp.concatenate([v, v], axis=-1)
    diag2 = lambda w: jnp.concatenate([jnp.concatenate([w, jnp.zeros_like(w)], axis=1),
                                       jnp.concatenate([jnp.zeros_like(w), w], axis=1)], axis=0)
    args = (z, diag2(w1p), twice(b1.reshape(1, hid)), diag2(w2), twice(b2.reshape(1, hid)),
            diag2(w3), twice(b3.reshape(1, hid)), diag2(w4), twice(freq), deltas)
    return pl.pallas_call(
        functools.partial(_filter_kernel, rows=rows, hw=hw),
        grid=(length // rows,),
        in_specs=[pl.BlockSpec((rows, EMB_PAD), lambda i: (i, 0))] + [_const_spec(a.shape) for a in args[1:]],
        out_specs=[out, out],
        out_shape=[jax.ShapeDtypeStruct((length, hw), F32)] * 2,
        compiler_params=_cparams(1),
        name="hyena_filter",
    )(*args)


def _dft_tables(seq):
    n2 = DFT_N2
    h1 = seq // n2
    n1 = 2 * h1
    hk = n1 // 2
    n = n1 * n2
    k1 = np.arange(hk)[:, None]
    t1 = np.arange(h1)[None, :]
    a1 = 2.0 * np.pi * ((k1 * t1) % n1) / n1
    nyquist = np.where(t1 % 2 == 0, 1.0, -1.0)
    twice = np.where(k1 == 0, 1.0, 2.0)
    f1_im, f3_im = -np.sin(a1), -twice * np.sin(a1)
    f1_im[0:1], f3_im[0:1] = nyquist, nyquist
    f1 = np.concatenate([np.cos(a1), f1_im], axis=0)
    f3 = np.concatenate([(twice * np.cos(a1)).T, f3_im.T], axis=1) / n
    kk1 = np.arange(hk + 1)[:, None, None]
    k2 = np.arange(n2)[None, :, None]
    t2 = np.arange(n2)[None, None, :]
    th = 2.0 * np.pi * ((t2 * k2 * n1 + t2 * kk1) % n) / n
    wr, wi = np.cos(th), -np.sin(th)
    m2f = np.concatenate([np.concatenate([wr, -wi], axis=2),
                          np.concatenate([wi, wr], axis=2)], axis=1)
    m2i = np.transpose(m2f, (0, 2, 1))
    as_bf16 = lambda a: jnp.asarray(a, F32).astype(BF16)
    return as_bf16(f1), as_bf16(m2f), as_bf16(m2i), as_bf16(f3)


def _pitch(rows):
    return rows + F32_SUBLANES


def _dft_stage1(load_rows, f1_ref, a_ref, n1):
    def body(t2, carry):
        p = _dot(f1_ref[...], load_rows(t2).astype(BF16))
        a_ref[pl.ds(pl.multiple_of(t2 * _pitch(n1), F32_SUBLANES), n1), :] = p
        return carry
    lax.fori_loop(0, DFT_N2, body, 0, unroll=DFT_UNROLL)


def _dft_stage2(a_ref, m2f_ref, k1, n1):
    hk = n1 // 2
    if k1 is None:
        ar = a_ref[pl.ds(hk, DFT_N2, stride=_pitch(n1)), :]
        ai = jnp.zeros_like(ar)
        k1 = hk
    else:
        ar = a_ref[pl.ds(k1, DFT_N2, stride=_pitch(n1)), :]
        ai = jnp.where(k1 == 0, 0.0, a_ref[pl.ds(hk + k1, DFT_N2, stride=_pitch(n1)), :])
    return _dot(m2f_ref[k1], jnp.concatenate([ar, ai], axis=0).astype(BF16))


def _spectrum_kernel(hf_ref, hb_ref, f1_ref, m2f_ref, kf_ref, af_ref, ab_ref, *, n1):
    n2 = DFT_N2
    hk = n1 // 2
    _dft_stage1(lambda t2: hf_ref[pl.ds(t2, hk, stride=n2), :], f1_ref, af_ref, n1)
    _dft_stage1(lambda t2: hb_ref[pl.ds(t2, hk, stride=n2), :], f1_ref, ab_ref, n1)

    def spectrum(k1):
        xf = _dft_stage2(af_ref, m2f_ref, k1, n1)
        xb = _dft_stage2(ab_ref, m2f_ref, k1, n1)
        sign = jnp.where(lax.broadcasted_iota(jnp.int32, (2 * n2, 1), 0) < n2, 1.0, -1.0)
        return xf + sign * xb

    def body(k1, carry):
        kf_ref[k1] = spectrum(k1)
        return carry
    lax.fori_loop(0, hk, body, 0, unroll=DFT_UNROLL)
    kf_ref[hk] = spectrum(None)


def _filter_spectrum(hf, hb, f1, m2f):
    s, hw = hf.shape
    n2 = DFT_N2
    n1 = 2 * s // n2
    cb = LANES
    col = pl.BlockSpec((s, cb), lambda j: (0, j))
    return pl.pallas_call(
        functools.partial(_spectrum_kernel, n1=n1),
        grid=(hw // cb,),
        in_specs=[col, col, _const_spec(f1.shape), _const_spec(m2f.shape)],
        out_specs=pl.BlockSpec((n1 // 2 + 1, 2 * n2, cb), lambda j: (0, 0, j)),
        out_shape=jax.ShapeDtypeStruct((n1 // 2 + 1, 2 * n2, hw), F32),
        scratch_shapes=[pltpu.VMEM((n2 * _pitch(n1), cb), F32), pltpu.VMEM((n2 * _pitch(n1), cb), F32)],
        compiler_params=_cparams(1),
        name="filter_spectrum",
    )(hf, hb, f1, m2f)


def _longconv_kernel(x0_ref, xv_ref, kf_ref, bias_ref, f1_ref, m2f_ref, m2i_ref, f3_ref, o_ref,
                     xs_ref, a_ref, b_ref, y_ref, *, n1):
    n2 = DFT_N2
    h1 = hk = n1 // 2

    def pitch_rows(t1, carry):
        xs_ref[pl.ds(pl.multiple_of(t1 * _pitch(n2), F32_SUBLANES), n2), :] = (
            xv_ref[0, pl.ds(pl.multiple_of(t1 * n2, n2), n2), :])
        return carry
    lax.fori_loop(0, h1, pitch_rows, 0, unroll=DFT_UNROLL)
    _dft_stage1(lambda t2: xs_ref[pl.ds(t2, h1, stride=_pitch(n2)), :], f1_ref, a_ref, n1)

    def filtered(k1, row):
        x = _dft_stage2(a_ref, m2f_ref, k1, n1)
        kf = kf_ref[row]
        xr, xi = x[0:n2], x[n2:2 * n2]
        kr, ki = kf[0:n2], kf[n2:2 * n2]
        y = jnp.concatenate([xr * kr - xi * ki, xr * ki + xi * kr], axis=0).astype(BF16)
        return _dot(m2i_ref[row], y)

    def per_k1(k1, carry):
        b_ref[pl.ds(pl.multiple_of(k1 * _pitch(2 * n2), F32_SUBLANES), 2 * n2), :] = filtered(k1, k1)
        return carry
    lax.fori_loop(0, hk, per_k1, 0, unroll=DFT_UNROLL)
    b_ref[pl.ds(n2, n2), :] = filtered(None, hk)[0:n2]

    def per_t2(t2, carry):
        br = b_ref[pl.ds(t2, hk, stride=_pitch(2 * n2)), :]
        bi = b_ref[pl.ds(n2 + t2, hk, stride=_pitch(2 * n2)), :]
        y = _dot(f3_ref[...], jnp.concatenate([br, bi], axis=0).astype(BF16))
        y_ref[pl.ds(pl.multiple_of(t2 * _pitch(h1), F32_SUBLANES), h1), :] = y
        return carry
    lax.fori_loop(0, n2, per_t2, 0, unroll=DFT_UNROLL)

    def per_t1(t1, carry):
        rows = pl.ds(pl.multiple_of(t1 * n2, n2), n2)
        y = y_ref[pl.ds(t1, n2, stride=_pitch(h1)), :]
        o_ref[0, rows, :] = (x0_ref[0, rows, :] * (y + xv_ref[0, rows, :] * bias_ref[...])).astype(BF16)
        return carry
    lax.fori_loop(0, h1, per_t1, 0, unroll=DFT_UNROLL)


def _longconv(x0, xv, kf, bias, f1, m2f, m2i, f3):
    b, s, hw = x0.shape
    n2 = DFT_N2
    n1 = 2 * s // n2
    cb = LANES
    tok = pl.BlockSpec((1, s, cb), lambda j, bb: (bb, 0, j))
    return pl.pallas_call(
        functools.partial(_longconv_kernel, n1=n1),
        grid=(hw // cb, b),
        in_specs=[tok, tok,
                  pl.BlockSpec((n1 // 2 + 1, 2 * n2, cb), lambda j, bb: (0, 0, j), pipeline_mode=pl.Buffered(1)),
                  pl.BlockSpec((1, cb), lambda j, bb: (0, j)),
                  _const_spec(f1.shape), _const_spec(m2f.shape), _const_spec(m2i.shape),
                  _const_spec(f3.shape)],
        out_specs=tok,
        out_shape=jax.ShapeDtypeStruct((b, s, hw), BF16),
        scratch_shapes=[pltpu.VMEM((n1 // 2 * _pitch(n2), cb), F32), pltpu.VMEM((n2 * _pitch(n1), cb), F32),
                        pltpu.VMEM((n1 // 2 * _pitch(2 * n2), cb), F32), pltpu.VMEM((n2 * _pitch(n1 // 2), cb), F32)],
        compiler_params=_cparams(2),
        name="longconv",
    )(x0, xv, kf, bias.reshape(1, hw), f1, m2f, m2i, f3)


NA_QROWS = NA_WIN_ROWS // 2
NA_BAND_ROWS = NA_QROWS + NA_WIN_ROWS


def _bias_kernel(rpb_ref, o_ref, *, n_dc):
    h = pl.program_id(0)
    n_dr = 2 * NA_WIN_ROWS - 1
    qcol = lax.broadcasted_iota(jnp.int32, (GRID_W, LANES), 0)
    lane = lax.broadcasted_iota(jnp.int32, (GRID_W, LANES), 1)
    kcol = lane % GRID_W
    win = jnp.clip(qcol - NA_WIN_COLS // 2, 0, GRID_W - NA_WIN_COLS)
    col_ok = (kcol >= win) & (kcol < win + NA_WIN_COLS)
    low_half = lane < GRID_W
    lane_row = lax.broadcasted_iota(jnp.int32, (1, LANES), 1)
    toeplitz = []
    for dr in range(n_dr):
        vec = jnp.zeros((1, LANES), F32)
        for d in range(n_dc):
            vec = jnp.where(lane_row == d, rpb_ref[(h * n_dr + dr) * n_dc + d], vec)
        rows = jnp.broadcast_to(vec * LOG2_E, (GRID_W, LANES))
        toeplitz.append([pltpu.roll(rows, (LANES - (NA_WIN_COLS - 1) + half * GRID_W) % LANES, 1,
                                    stride=1, stride_axis=0) for half in range(2)])
    neg = jnp.full((GRID_W, LANES), NEG_BIG, F32)
    for pos in range(3):
        for p in range(NA_QROWS):
            lo = (0, p, NA_QROWS)[pos]
            for jj in range(NA_BAND_ROWS // 2):
                halves = []
                for half in range(2):
                    j = 2 * jj + half
                    dr = j - p + NA_WIN_ROWS - 1 - NA_QROWS * pos
                    halves.append(toeplitz[dr][half] if lo <= j < lo + NA_WIN_ROWS else neg)
                tile = jnp.where(low_half, halves[0], halves[1])
                o_ref[pos, 0, p * GRID_W:(p + 1) * GRID_W, jj * LANES:(jj + 1) * LANES] = (
                    jnp.where(col_ok, tile, NEG_BIG))


def _bias_table(rpb):
    heads, n_dr, n_dc = rpb.shape
    blk = (NA_QROWS * GRID_W, NA_BAND_ROWS * GRID_W)
    assert 2 * GRID_W == LANES and NA_BAND_ROWS % 2 == 0 and n_dc <= LANES
    return pl.pallas_call(
        functools.partial(_bias_kernel, n_dc=n_dc),
        grid=(heads,),
        in_specs=[pl.BlockSpec(memory_space=pltpu.SMEM)],
        out_specs=pl.BlockSpec((3, 1) + blk, lambda h: (0, h, 0, 0)),
        out_shape=jax.ShapeDtypeStruct((3, heads) + blk, F32),
        compiler_params=_cparams(1),
        name="na_bias",
    )(rpb.reshape(-1))


def _nattn_kernel(q_ref, k_ref, v_ref, kc_ref, vc_ref, bias_ref, o_ref, vs_ref, vcs_ref, *, rows):
    dh = NA_HEAD_DIM
    nq = NA_QROWS * GRID_W
    band = NA_BAND_ROWS * GRID_W
    nt = (((1,), (1,)), ((), ()))

    def own_lanes(n_rows):
        lane = lax.broadcasted_iota(jnp.int32, (n_rows, LANES), 1)
        return [lane < dh, lane >= dh]

    for hh in range(2):
        vs_ref[hh] = jnp.where(own_lanes(v_ref.shape[1])[hh], v_ref[0], 1.0).astype(BF16)
        vcs_ref[hh] = jnp.where(own_lanes(vc_ref.shape[1])[hh], vc_ref[0], 1.0).astype(BF16)
    own_q = own_lanes(nq)
    n_blocks = rows // NA_QROWS

    def per_block(i, carry):
        r0 = i * NA_QROWS
        pos = jnp.where(i == 0, 0, jnp.where(i == n_blocks - 1, 2, 1))
        key0 = pl.multiple_of(jnp.clip(r0 - NA_WIN_ROWS // 2, 0, rows - NA_BAND_ROWS) * GRID_W, GRID_W)
        qrows = pl.ds(pl.multiple_of(r0 * GRID_W, nq), nq)
        q = q_ref[0, qrows, :]
        kb = k_ref[0, pl.ds(key0, band), :]
        outs = []
        for hh in range(2):
            qh = jnp.where(own_q[hh], q, 0.0).astype(BF16)
            s_loc = lax.dot_general(qh, kb, nt, preferred_element_type=F32) + bias_ref[pos, hh]
            s_ctx = lax.dot_general(qh, kc_ref[0], nt, preferred_element_type=F32)
            s = jnp.concatenate([s_loc, s_ctx], axis=-1)
            p = jnp.exp2(s - jnp.max(s, axis=-1, keepdims=True)).astype(BF16)
            o = _dot(p[:, 0:band], vs_ref[hh, pl.ds(key0, band), :]) + _dot(p[:, band:], vcs_ref[hh])
            outs.append(o / pltpu.roll(o, dh, axis=1))
        o_ref[0, qrows, :] = jnp.where(own_q[0], outs[0], outs[1]).astype(BF16)
        return carry
    lax.fori_loop(0, n_blocks, per_block, 0, unroll=NA_UNROLL)


def _nattn(q, k, v, kc, vc, bias):
    b, s, aw = q.shape
    nctx = kc.shape[1]
    hps = LANES // NA_HEAD_DIM
    assert (s // GRID_W) % NA_QROWS == 0 and s // GRID_W >= NA_BAND_ROWS
    tok = pl.BlockSpec((1, s, LANES), lambda j, bb: (bb, 0, j))
    ctx = pl.BlockSpec((1, nctx, LANES), lambda j, bb: (bb, 0, j))
    return pl.pallas_call(
        functools.partial(_nattn_kernel, rows=s // GRID_W),
        grid=(aw // LANES, b),
        in_specs=[tok, tok, tok, ctx, ctx,
                  pl.BlockSpec((3, hps) + bias.shape[2:], lambda j, bb: (0, j, 0, 0))],
        out_specs=tok,
        out_shape=jax.ShapeDtypeStruct((b, s, aw), BF16),
        scratch_shapes=[pltpu.VMEM((hps, s, LANES), BF16), pltpu.VMEM((hps, nctx, LANES), BF16)],
        compiler_params=_cparams(2),
        name="nattn",
    )(q, k, v, kc, vc, bias)


def _outproj_kernel(yh_ref, ya_ref, h_ref, mod_ref, g_ref, w_ref, o_ref, *, d, hw):
    gate = mod_ref[0, :, 2 * d:3 * d]
    y = _dot(yh_ref[0], w_ref[0:hw, :]) + _dot(ya_ref[0], w_ref[hw:, :])
    o_ref[0] = h_ref[0] + gate * _rms(y, g_ref[...])


def _outproj(y_hy, y_na, h, mod, g, w_out, tile):
    b, s, d = h.shape
    hw = y_hy.shape[2]
    aw = y_na.shape[2]
    tok = lambda width: pl.BlockSpec((1, tile, width), lambda bb, i: (bb, i, 0))
    return pl.pallas_call(
        functools.partial(_outproj_kernel, d=d, hw=hw),
        grid=(b, s // tile),
        in_specs=[tok(hw), tok(aw), tok(d),
                  pl.BlockSpec((1, 1, mod.shape[2]), lambda bb, i: (bb, 0, 0)),
                  _const_spec((1, d)), _const_spec(w_out.shape)],
        out_specs=tok(d),
        out_shape=jax.ShapeDtypeStruct((b, s, d), F32),
        compiler_params=_cparams(2),
        name="outproj",
    )(y_hy, y_na, h, mod, g.reshape(1, d), w_out)


def _ffn_kernel(prev_ref, main_ref, next_ref, mod_ref, gpre_ref, gpost_ref, wup_ref, cw_ref, cb_ref, wdn_ref,
                o_ref, hn_ref, z_ref, acc_ref, *, tile, d, n_chunks):
    fc = FFN_CHUNK
    shift = mod_ref[0, :, 3 * d:4 * d]
    scale = mod_ref[0, :, 4 * d:5 * d]
    gate = mod_ref[0, :, 5 * d:6 * d]
    g = gpre_ref[...]
    _fill_halo_tile(hn_ref, prev_ref, main_ref, next_ref,
                    lambda t: _mod_norm(t, g, scale, shift), tile)
    acc_ref[...] = jnp.zeros_like(acc_ref)

    def chunk_start(j, base):
        start = base + j * fc
        return start if isinstance(j, int) else pl.multiple_of(start, LANES)

    def up(j, slot):
        z_ref[slot, :, 0:fc] = _dot(hn_ref[...], wup_ref[:, pl.ds(chunk_start(j, 0), fc)])
        z_ref[slot, :, fc:2 * fc] = _dot(hn_ref[...], wup_ref[:, pl.ds(chunk_start(j, n_chunks * fc), fc)])

    def down(j, slot):
        cw = cw_ref[j]
        c = (cw[0:1, :] * z_ref[slot, pl.ds(HALO - 1, tile), :]
             + cw[1:2, :] * z_ref[slot, pl.ds(HALO, tile), :]
             + cw[2:3, :] * z_ref[slot, pl.ds(HALO + 1, tile), :]
             + cb_ref[j])
        a = (_silu(c[:, 0:fc]) * c[:, fc:2 * fc]).astype(BF16)
        acc_ref[...] += _dot(a, wdn_ref[pl.ds(chunk_start(j, 0), fc), :])

    def step(j, slot):
        up(j + 1, 1 - slot)
        down(j, slot)

    up(0, 0)
    n_pairs = (n_chunks - 1) // 2

    def pair(jj, carry):
        step(2 * jj, 0)
        step(2 * jj + 1, 1)
        return carry
    lax.fori_loop(0, n_pairs, pair, 0)
    if (n_chunks - 1) % 2:
        step(n_chunks - 2, 0)
    down(n_chunks - 1, (n_chunks - 1) % 2)
    o_ref[0] = main_ref[0] + gate * _rms(acc_ref[...], gpost_ref[...])


def _layer_spec(stacked, layer):
    nd = stacked.ndim - 1
    return pl.BlockSpec((None,) + stacked.shape[1:], lambda *_: (layer,) + (0,) * nd, pipeline_mode=pl.Buffered(1))


def _ffn(h, mod, g_pre, g_post, wup_all, w_dw, b_dw, wdn_all, layer, tile):
    b, s, d = h.shape
    dff = wdn_all.shape[1]
    fc = FFN_CHUNK
    nch = dff // fc
    pair = lambda t: jnp.concatenate([t[..., :dff].reshape(t.shape[:-1] + (nch, fc)),
                                      t[..., dff:].reshape(t.shape[:-1] + (nch, fc))], axis=-1)
    cw = jnp.transpose(pair(w_dw), (1, 0, 2))
    cb = pair(b_dw).reshape(nch, 1, 2 * fc)
    return pl.pallas_call(
        functools.partial(_ffn_kernel, tile=tile, d=d, n_chunks=nch),
        grid=(b, s // tile),
        in_specs=_halo_specs(tile, d, s) + [
            pl.BlockSpec((1, 1, mod.shape[2]), lambda bb, i: (bb, 0, 0)),
            _const_spec((1, d)), _const_spec((1, d)),
            _layer_spec(wup_all, layer), _const_spec(cw.shape), _const_spec(cb.shape), _layer_spec(wdn_all, layer)],
        out_specs=pl.BlockSpec((1, tile, d), lambda bb, i: (bb, i, 0)),
        out_shape=jax.ShapeDtypeStruct((b, s, d), F32),
        scratch_shapes=[pltpu.VMEM((tile + 2 * HALO, d), BF16),
                        pltpu.VMEM((2, tile + 2 * HALO, 2 * fc), F32),
                        pltpu.VMEM((tile, d), F32)],
        compiler_params=_cparams(2),
        name="convffn",
    )(h, h, h, mod, g_pre.reshape(1, d), g_post.reshape(1, d), wup_all, cw, cb, wdn_all)


CONF_TOKENS = 128


def _conformer_kernel(prev_ref, main_ref, next_ref, mod_ref, gpre_ref, gpost_ref, w1_ref, b1_ref,
                      dw_ref, db_ref, lg_ref, lb_ref, w2_ref, b2_ref, o_ref, hn_ref, u_ref, c_ref,
                      *, tile, d, seq):
    shift = mod_ref[0, :, 0:d]
    scale = mod_ref[0, :, d:2 * d]
    gate = mod_ref[0, :, 2 * d:3 * d]
    g = gpre_ref[...]
    _fill_halo_tile(hn_ref, prev_ref, main_ref, next_ref,
                    lambda t: _mod_norm(t, g, scale, shift), tile)
    ag = _dot(hn_ref[...], w1_ref[...]) + b1_ref[...]
    pos = (pl.program_id(1) * tile - HALO
           + lax.broadcasted_iota(jnp.int32, (tile + 2 * HALO, 1), 0))
    u = jnp.where((pos >= 0) & (pos < seq), ag[:, 0:d] * jax.nn.sigmoid(ag[:, d:2 * d]), 0.0)
    groups = d // LANES
    n_rows = tile + 2 * HALO
    for j in range(groups):
        u_ref[pl.ds(j, n_rows, stride=groups), :] = u[:, j * LANES:(j + 1) * LANES]
    first = HALO - CONF_K // 2

    def token_tile(ref, t):
        return ref[pl.ds(pl.multiple_of(t * groups, groups), groups), :]

    def conv_block(tb, carry):
        t0 = tb * CONF_TOKENS
        for i in range(CONF_TOKENS):
            acc = db_ref[...]
            for k in range(CONF_K):
                acc = acc + dw_ref[k] * token_tile(u_ref, t0 + i + first + k)
            c_ref[pl.ds(pl.multiple_of((t0 + i) * groups, groups), groups), :] = acc
        return carry
    lax.fori_loop(0, tile // CONF_TOKENS, conv_block, 0)
    c = jnp.concatenate([c_ref[pl.ds(j, tile, stride=groups), :] for j in range(groups)], axis=1)
    mu = jnp.mean(c, axis=-1, keepdims=True)
    cc = c - mu
    var = jnp.mean(cc * cc, axis=-1, keepdims=True)
    ln = cc * lax.rsqrt(var + EPS) * lg_ref[...] + lb_ref[...]
    y = _dot(_silu(ln).astype(BF16), w2_ref[...]) + b2_ref[...]
    o_ref[0] = main_ref[0] + gate * _rms(y, gpost_ref[...])


def _conformer(h, mod, g_pre, g_post, w_pw1, b_pw1, w_dw, b_dw, ln_g, ln_b, w_pw2, b_pw2, tile):
    b, s, d = h.shape
    groups = d // LANES
    assert groups == F32_SUBLANES and tile % CONF_TOKENS == 0
    row = lambda t: t.reshape(1, -1)
    return pl.pallas_call(
        functools.partial(_conformer_kernel, tile=tile, d=d, seq=s),
        grid=(b, s // tile),
        in_specs=_halo_specs(tile, d, s) + [
            pl.BlockSpec((1, 1, mod.shape[2]), lambda bb, i: (bb, 0, 0)),
            _const_spec((1, d)), _const_spec((1, d)),
            _const_spec(w_pw1.shape), _const_spec((1, 2 * d)),
            _const_spec((CONF_K, groups, LANES)), _const_spec((groups, LANES)),
            _const_spec((1, d)), _const_spec((1, d)),
            _const_spec(w_pw2.shape), _const_spec((1, d))],
        out_specs=pl.BlockSpec((1, tile, d), lambda bb, i: (bb, i, 0)),
        out_shape=jax.ShapeDtypeStruct((b, s, d), F32),
        scratch_shapes=[pltpu.VMEM((tile + 2 * HALO, d), BF16),
                        pltpu.VMEM(((tile + 2 * HALO) * groups, LANES), F32),
                        pltpu.VMEM((tile * groups, LANES), F32)],
        compiler_params=_cparams(2),
        name="conformer",
    )(h, h, h, mod, row(g_pre), row(g_post), w_pw1.astype(BF16), row(b_pw1),
      w_dw.reshape(CONF_K, groups, LANES), b_dw.reshape(groups, LANES),
      row(ln_g), row(ln_b), w_pw2.astype(BF16), row(b_pw2))


def _token_tile(seq):
    return min(seq, 1024)


def kernel(x, c, ctx, c_ctx, w_mod, b_mod, g_mix_pre, g_mix_post, g_ffn_pre, g_ffn_post, w_in, w_out, hy_short_w, hy_short_b, hy_f_w1, hy_f_b1, hy_f_w2, hy_f_b2, hy_f_w3, hy_f_b3, hy_f_w4, hy_f_freq, hy_bias, na_rpb, cf_w_pw1, cf_b_pw1, cf_w_dw, cf_b_dw, cf_ln_g, cf_ln_b, cf_w_pw2, cf_b_pw2, ffn_w_up, ffn_w_dw, ffn_b_dw, ffn_w_down):
    bsz, seq, d = x.shape
    depth = w_mod.shape[0]
    hw = hy_bias.shape[1]
    aw = NA_HEADS * NA_HEAD_DIM
    tile = _token_tile(seq)
    assert seq % tile == 0 and seq % (GRID_W * NA_WIN_ROWS) == 0 and tile % HALO == 0

    n_rows = -(-(bsz + 1) // BF16_SUBLANES) * BF16_SUBLANES
    cc = jnp.concatenate([c, c_ctx[None, :], jnp.zeros((n_rows - bsz - 1, d), F32)], axis=0)
    mod_all = _adaln(cc, w_mod, b_mod)
    ffn_wup, ffn_wdn = ffn_w_up.astype(BF16), ffn_w_down.astype(BF16)

    h = x
    for layer in range(depth):
        mod = mod_all[layer, :bsz].reshape(bsz, 1, 6 * d)
        if layer % 2 == 0:
            e = layer // 2
            w_e = w_in[e].astype(BF16)
            k_ctx, v_ctx = _ctxkv(ctx, mod_all[layer, bsz:bsz + 1, :2 * d], g_mix_pre[layer],
                                  w_e[:, 3 * hw + aw:])
            x0, xv, q, k, v = _inproj(h, mod, g_mix_pre[layer], w_e[:, :3 * hw], w_e[:, 3 * hw:],
                                      hy_short_w[e], hy_short_b[e], tile)
            f1, m2f, m2i, f3 = _dft_tables(seq)
            hf, hb = _hyena_filters(seq, hy_f_w1[e], hy_f_b1[e], hy_f_w2[e], hy_f_b2[e],
                                    hy_f_w3[e], hy_f_b3[e], hy_f_w4[e], hy_f_freq[e])
            kf = _filter_spectrum(hf, hb, f1, m2f)
            y_hy = _longconv(x0, xv, kf, hy_bias[e], f1, m2f, m2i, f3)
            y_na = _nattn(q, k, v, k_ctx, v_ctx, _bias_table(na_rpb[e]))
            h = _outproj(y_hy, y_na, h, mod, g_mix_post[layer], w_out[e].astype(BF16), tile)
        else:
            o = layer // 2
            h = _conformer(h, mod, g_mix_pre[layer], g_mix_post[layer], cf_w_pw1[o], cf_b_pw1[o],
                           cf_w_dw[o], cf_b_dw[o], cf_ln_g[o], cf_ln_b[o], cf_w_pw2[o], cf_b_pw2[o], tile)
        h = _ffn(h, mod, g_ffn_pre[layer], g_ffn_post[layer], ffn_wup, ffn_w_dw[layer],
                 ffn_b_dw[layer], ffn_wdn, layer, tile)
    return h
```

```python
import functools
import math

import numpy as np
import jax
import jax.numpy as jnp
from jax import lax
from jax.experimental import pallas as pl
from jax.experimental.pallas import tpu as pltpu

F32 = jnp.float32
BF16 = jnp.bfloat16

EPS = 1e-6
NEG_BIG = -1e30
LOG2_E = math.log2(math.e)

GRID_W = 64
NA_HEADS = 8
NA_HEAD_DIM = 64
NA_WIN_ROWS = 8
NA_WIN_COLS = 16
HY_SHORT_K = 3
HY_EMB_DIM = 33
HY_DECAY_FAST = 0.3
HY_DECAY_SLOW = 1.5
HY_DECAY_TARGET = 1e-2
CONF_K = 31
FFN_CONV_K = 3

LANES = 128
F32_SUBLANES = 8
BF16_SUBLANES = 16
VMEM_LIMIT = 56 * 1024 * 1024

HALO = BF16_SUBLANES
DFT_N2 = 64
DFT_UNROLL = True
NA_UNROLL = True
FFN_CHUNK = 256
EMB_PAD = LANES


def _cparams(n_axes):
    return pltpu.CompilerParams(dimension_semantics=("arbitrary",) * n_axes,
                                vmem_limit_bytes=VMEM_LIMIT)


def _const_spec(shape):
    nd = len(shape)
    return pl.BlockSpec(shape, lambda *_: (0,) * nd, pipeline_mode=pl.Buffered(1))


def _silu(t):
    return t * jax.nn.sigmoid(t)


def _mod_norm(t, g, scale, shift):
    ms = jnp.mean(t * t, axis=-1, keepdims=True)
    return (t * lax.rsqrt(ms + EPS) * g) * (1.0 + scale) + shift


def _rms(t, g):
    ms = jnp.mean(t * t, axis=-1, keepdims=True)
    return t * lax.rsqrt(ms + EPS) * g


def _dot(a, b):
    return jnp.dot(a, b, preferred_element_type=F32)


def _dot_hi(a, b):
    return jnp.dot(a, b, preferred_element_type=F32, precision=lax.Precision.HIGHEST)


def _fill_halo_tile(hn_ref, prev_ref, main_ref, next_ref, norm_fn, tile):
    i = pl.program_id(1)
    last = pl.num_programs(1) - 1
    p = norm_fn(prev_ref[0])
    n = norm_fn(next_ref[0])
    hn_ref[0:HALO, :] = jnp.where(i > 0, p, 0.0).astype(BF16)
    hn_ref[HALO:HALO + tile, :] = norm_fn(main_ref[0]).astype(BF16)
    hn_ref[HALO + tile:, :] = jnp.where(i < last, n, 0.0).astype(BF16)


def _halo_specs(tile, d, seq):
    r = tile // HALO
    nblk = seq // HALO
    return [
        pl.BlockSpec((1, HALO, d), lambda b, i: (b, jnp.maximum(i * r - 1, 0), 0)),
        pl.BlockSpec((1, tile, d), lambda b, i: (b, i, 0)),
        pl.BlockSpec((1, HALO, d), lambda b, i: (b, jnp.minimum((i + 1) * r, nblk - 1), 0)),
    ]


def _adaln_kernel(c_ref, w_ref, b_ref, o_ref):
    s = _silu(c_ref[...]).astype(BF16)
    o_ref[0] = _dot(s, w_ref[0].astype(BF16)) + b_ref[0]


def _adaln(cc, w_mod, b_mod):
    depth, d, n6 = w_mod.shape
    r = cc.shape[0]
    nb = d
    return pl.pallas_call(
        _adaln_kernel,
        grid=(depth, n6 // nb),
        in_specs=[pl.BlockSpec((r, d), lambda l, j: (0, 0)),
                  pl.BlockSpec((1, d, nb), lambda l, j: (l, 0, j)),
                  pl.BlockSpec((1, 1, nb), lambda l, j: (l, 0, j))],
        out_specs=pl.BlockSpec((1, r, nb), lambda l, j: (l, 0, j)),
        out_shape=jax.ShapeDtypeStruct((depth, r, n6), F32),
        compiler_params=_cparams(2),
        name="adaln",
    )(cc, w_mod, b_mod.reshape(depth, 1, n6))


def _inproj_kernel(prev_ref, main_ref, next_ref, mod_ref, g_ref, why_ref, wqkv_ref, sw_ref, sb_ref,
                   x0_ref, xv_ref, q_ref, k_ref, v_ref, hn_ref, z_ref, *, tile, d, hw, aw):
    shift = mod_ref[0, :, 0:d]
    scale = mod_ref[0, :, d:2 * d]
    g = g_ref[...]
    _fill_halo_tile(hn_ref, prev_ref, main_ref, next_ref,
                    lambda t: _mod_norm(t, g, scale, shift), tile)
    z_ref[...] = _dot(hn_ref[...], why_ref[...])
    uc = (sw_ref[0:1, :] * z_ref[pl.ds(HALO - 1, tile), :]
          + sw_ref[1:2, :] * z_ref[pl.ds(HALO, tile), :]
          + sw_ref[2:3, :] * z_ref[pl.ds(HALO + 1, tile), :]
          + sb_ref[...])
    x0_ref[0] = uc[:, 0:hw]
    xv_ref[0] = uc[:, hw:2 * hw] * uc[:, 2 * hw:3 * hw]
    qkv = _dot(hn_ref[pl.ds(HALO, tile), :], wqkv_ref[...])
    q_ref[0] = (qkv[:, 0:aw] * (NA_HEAD_DIM ** -0.5 * LOG2_E)).astype(BF16)
    k_ref[0] = qkv[:, aw:2 * aw].astype(BF16)
    v_ref[0] = qkv[:, 2 * aw:3 * aw].astype(BF16)


def _inproj(h, mod, g, w_hy, w_qkv, short_w, short_b, tile):
    b, s, d = h.shape
    hw = w_hy.shape[1] // 3
    aw = w_qkv.shape[1] // 3
    tok = lambda width: pl.BlockSpec((1, tile, width), lambda bb, i: (bb, i, 0))
    return pl.pallas_call(
        functools.partial(_inproj_kernel, tile=tile, d=d, hw=hw, aw=aw),
        grid=(b, s // tile),
        in_specs=_halo_specs(tile, d, s) + [
            pl.BlockSpec((1, 1, mod.shape[2]), lambda bb, i: (bb, 0, 0)),
            _const_spec((1, d)),
            _const_spec(w_hy.shape),
            _const_spec(w_qkv.shape),
            _const_spec(short_w.shape),
            _const_spec((1, 3 * hw)),
        ],
        out_specs=[tok(hw), tok(hw), tok(aw), tok(aw), tok(aw)],
        out_shape=[jax.ShapeDtypeStruct((b, s, hw), F32), jax.ShapeDtypeStruct((b, s, hw), F32),
                   jax.ShapeDtypeStruct((b, s, aw), BF16), jax.ShapeDtypeStruct((b, s, aw), BF16),
                   jax.ShapeDtypeStruct((b, s, aw), BF16)],
        scratch_shapes=[pltpu.VMEM((tile + 2 * HALO, d), BF16),
                        pltpu.VMEM((tile + 2 * HALO, 3 * hw), F32)],
        compiler_params=_cparams(2),
        name="inproj",
    )(h, h, h, mod, g.reshape(1, d), w_hy, w_qkv, short_w, short_b.reshape(1, 3 * hw))


def _ctxkv_kernel(ctx_ref, mod_ref, g_ref, w_ref, k_ref, v_ref, *, d, aw):
    shift = mod_ref[:, 0:d]
    scale = mod_ref[:, d:2 * d]
    cn = _mod_norm(ctx_ref[0], g_ref[...], scale, shift).astype(BF16)
    kv = _dot(cn, w_ref[...])
    k_ref[0] = kv[:, 0:aw].astype(BF16)
    v_ref[0] = kv[:, aw:2 * aw].astype(BF16)


def _ctxkv(ctx, mod_ctx, g, w_kv):
    b, n, d = ctx.shape
    aw = w_kv.shape[1] // 2
    out = pl.BlockSpec((1, n, aw), lambda bb: (bb, 0, 0))
    return pl.pallas_call(
        functools.partial(_ctxkv_kernel, d=d, aw=aw),
        grid=(b,),
        in_specs=[pl.BlockSpec((1, n, d), lambda bb: (bb, 0, 0)),
                  _const_spec(mod_ctx.shape), _const_spec((1, d)), _const_spec(w_kv.shape)],
        out_specs=[out, out],
        out_shape=[jax.ShapeDtypeStruct((b, n, aw), BF16)] * 2,
        compiler_params=_cparams(1),
        name="ctxkv",
    )(ctx, mod_ctx, g.reshape(1, d), w_kv)


def _filter_kernel(z_ref, w1_ref, b1_ref, w2_ref, b2_ref, w3_ref, b3_ref, w4_ref, fr_ref, dl_ref,
                   hf_ref, hb_ref, *, rows, hw):
    half = rows // 2
    z = jnp.concatenate([z_ref[0:half, :], z_ref[half:rows, :]], axis=1)
    hdn = jnp.sin(fr_ref[0:1, :] * (_dot_hi(z, w1_ref[...]) + b1_ref[...]))
    hdn = jnp.sin(fr_ref[1:2, :] * (_dot_hi(hdn, w2_ref[...]) + b2_ref[...]))
    hdn = jnp.sin(fr_ref[2:3, :] * (_dot_hi(hdn, w3_ref[...]) + b3_ref[...]))
    filt = _dot_hi(hdn, w4_ref[...])
    for part in range(2):
        rs = slice(part * half, (part + 1) * half)
        f = filt[:, part * 2 * hw:(part + 1) * 2 * hw]
        decay = jnp.exp(-z_ref[rs, 0:1] * dl_ref[...])
        hf_ref[rs, :] = f[:, 0:hw] * decay
        pos = pl.program_id(0) * rows + part * half + lax.broadcasted_iota(jnp.int32, (half, 1), 0)
        hb_ref[rs, :] = jnp.where(pos > 0, f[:, hw:2 * hw] * decay, 0.0)


def _hyena_filters(length, w1, b1, w2, b2, w3, b3, w4, freq):
    hid = w1.shape[1]
    hw = w4.shape[1] // 2
    bands = (HY_EMB_DIM - 1) // 2
    t01 = np.linspace(0.0, 1.0, length)[:, None]
    ang = (2.0 * np.pi * np.arange(length) / length)[:, None] * np.linspace(1e-4, bands - 1, bands)[None, :]
    z = jnp.asarray(np.concatenate([t01, np.cos(ang), -np.sin(ang),
                                    np.zeros((length, EMB_PAD - HY_EMB_DIM))], axis=-1), F32)
    w1p = jnp.concatenate([w1, jnp.zeros((EMB_PAD - HY_EMB_DIM, hid), F32)], axis=0)
    deltas = jnp.asarray(np.abs(np.linspace(math.log(HY_DECAY_TARGET) / HY_DECAY_FAST,
                                            math.log(HY_DECAY_TARGET) / HY_DECAY_SLOW, hw))[None, :], F32)
    rows = min(length, 1024)
    out = pl.BlockSpec((rows, hw), lambda i: (i, 0))
    twice = lambda v: jnp.concatenate([v, v], axis=-1)
    diag2 = lambda w: jnp.concatenate([jnp.concatenate([w, jnp.zeros_like(w)], axis=1),
                                       jnp.concatenate([jnp.zeros_like(w), w], axis=1)], axis=0)
    args = (z, diag2(w1p), twice(b1.reshape(1, hid)), diag2(w2), twice(b2.reshape(1, hid)),
            diag2(w3), twice(b3.reshape(1, hid)), diag2(w4), twice(freq), deltas)
    return pl.pallas_call(
        functools.partial(_filter_kernel, rows=rows, hw=hw),
        grid=(length // rows,),
        in_specs=[pl.BlockSpec((rows, EMB_PAD), lambda i: (i, 0))] + [_const_spec(a.shape) for a in args[1:]],
        out_specs=[out, out],
        out_shape=[jax.ShapeDtypeStruct((length, hw), F32)] * 2,
        compiler_params=_cparams(1),
        name="hyena_filter",
    )(*args)


def _dft_tables(seq):
    n2 = DFT_N2
    h1 = seq // n2
    n1 = 2 * h1
    hk = n1 // 2
    n = n1 * n2
    k1 = np.arange(hk)[:, None]
    t1 = np.arange(h1)[None, :]
    a1 = 2.0 * np.pi * ((k1 * t1) % n1) / n1
    nyquist = np.where(t1 % 2 == 0, 1.0, -1.0)
    twice = np.where(k1 == 0, 1.0, 2.0)
    f1_im, f3_im = -np.sin(a1), -twice * np.sin(a1)
    f1_im[0:1], f3_im[0:1] = nyquist, nyquist
    f1 = np.concatenate([np.cos(a1), f1_im], axis=0)
    f3 = np.concatenate([(twice * np.cos(a1)).T, f3_im.T], axis=1) / n
    kk1 = np.arange(hk + 1)[:, None, None]
    k2 = np.arange(n2)[None, :, None]
    t2 = np.arange(n2)[None, None, :]
    th = 2.0 * np.pi * ((t2 * k2 * n1 + t2 * kk1) % n) / n
    wr, wi = np.cos(th), -np.sin(th)
    m2f = np.concatenate([np.concatenate([wr, -wi], axis=2),
                          np.concatenate([wi, wr], axis=2)], axis=1)
    m2i = np.transpose(m2f, (0, 2, 1))
    as_bf16 = lambda a: jnp.asarray(a, F32).astype(BF16)
    return as_bf16(f1), as_bf16(m2f), as_bf16(m2i), as_bf16(f3)


def _pitch(rows):
    return rows + F32_SUBLANES


def _dft_stage1(load_rows, f1_ref, a_ref, n1):
    def body(t2, carry):
        p = _dot(f1_ref[...], load_rows(t2).astype(BF16))
        a_ref[pl.ds(pl.multiple_of(t2 * _pitch(n1), F32_SUBLANES), n1), :] = p
        return carry
    lax.fori_loop(0, DFT_N2, body, 0, unroll=DFT_UNROLL)


def _dft_stage2(a_ref, m2f_ref, k1, n1):
    hk = n1 // 2
    if k1 is None:
        ar = a_ref[pl.ds(hk, DFT_N2, stride=_pitch(n1)), :]
        ai = jnp.zeros_like(ar)
        k1 = hk
    else:
        ar = a_ref[pl.ds(k1, DFT_N2, stride=_pitch(n1)), :]
        ai = jnp.where(k1 == 0, 0.0, a_ref[pl.ds(hk + k1, DFT_N2, stride=_pitch(n1)), :])
    return _dot(m2f_ref[k1], jnp.concatenate([ar, ai], axis=0).astype(BF16))


def _spectrum_kernel(hf_ref, hb_ref, f1_ref, m2f_ref, kf_ref, af_ref, ab_ref, *, n1):
    n2 = DFT_N2
    hk = n1 // 2
    _dft_stage1(lambda t2: hf_ref[pl.ds(t2, hk, stride=n2), :], f1_ref, af_ref, n1)
    _dft_stage1(lambda t2: hb_ref[pl.ds(t2, hk, stride=n2), :], f1_ref, ab_ref, n1)

    def spectrum(k1):
        xf = _dft_stage2(af_ref, m2f_ref, k1, n1)
        xb = _dft_stage2(ab_ref, m2f_ref, k1, n1)
        sign = jnp.where(lax.broadcasted_iota(jnp.int32, (2 * n2, 1), 0) < n2, 1.0, -1.0)
        return xf + sign * xb

    def body(k1, carry):
        kf_ref[k1] = spectrum(k1)
        return carry
    lax.fori_loop(0, hk, body, 0, unroll=DFT_UNROLL)
    kf_ref[hk] = spectrum(None)


def _filter_spectrum(hf, hb, f1, m2f):
    s, hw = hf.shape
    n2 = DFT_N2
    n1 = 2 * s // n2
    cb = LANES
    col = pl.BlockSpec((s, cb), lambda j: (0, j))
    return pl.pallas_call(
        functools.partial(_spectrum_kernel, n1=n1),
        grid=(hw // cb,),
        in_specs=[col, col, _const_spec(f1.shape), _const_spec(m2f.shape)],
        out_specs=pl.BlockSpec((n1 // 2 + 1, 2 * n2, cb), lambda j: (0, 0, j)),
        out_shape=jax.ShapeDtypeStruct((n1 // 2 + 1, 2 * n2, hw), F32),
        scratch_shapes=[pltpu.VMEM((n2 * _pitch(n1), cb), F32), pltpu.VMEM((n2 * _pitch(n1), cb), F32)],
        compiler_params=_cparams(1),
        name="filter_spectrum",
    )(hf, hb, f1, m2f)


def _longconv_kernel(x0_ref, xv_ref, kf_ref, bias_ref, f1_ref, m2f_ref, m2i_ref, f3_ref, o_ref,
                     xs_ref, a_ref, b_ref, y_ref, *, n1):
    n2 = DFT_N2
    h1 = hk = n1 // 2

    def pitch_rows(t1, carry):
        xs_ref[pl.ds(pl.multiple_of(t1 * _pitch(n2), F32_SUBLANES), n2), :] = (
            xv_ref[0, pl.ds(pl.multiple_of(t1 * n2, n2), n2), :])
        return carry
    lax.fori_loop(0, h1, pitch_rows, 0, unroll=DFT_UNROLL)
    _dft_stage1(lambda t2: xs_ref[pl.ds(t2, h1, stride=_pitch(n2)), :], f1_ref, a_ref, n1)

    def filtered(k1, row):
        x = _dft_stage2(a_ref, m2f_ref, k1, n1)
        kf = kf_ref[row]
        xr, xi = x[0:n2], x[n2:2 * n2]
        kr, ki = kf[0:n2], kf[n2:2 * n2]
        y = jnp.concatenate([xr * kr - xi * ki, xr * ki + xi * kr], axis=0).astype(BF16)
        return _dot(m2i_ref[row], y)

    def per_k1(k1, carry):
        b_ref[pl.ds(pl.multiple_of(k1 * _pitch(2 * n2), F32_SUBLANES), 2 * n2), :] = filtered(k1, k1)
        return carry
    lax.fori_loop(0, hk, per_k1, 0, unroll=DFT_UNROLL)
    b_ref[pl.ds(n2, n2), :] = filtered(None, hk)[0:n2]

    def per_t2(t2, carry):
        br = b_ref[pl.ds(t2, hk, stride=_pitch(2 * n2)), :]
        bi = b_ref[pl.ds(n2 + t2, hk, stride=_pitch(2 * n2)), :]
        y = _dot(f3_ref[...], jnp.concatenate([br, bi], axis=0).astype(BF16))
        y_ref[pl.ds(pl.multiple_of(t2 * _pitch(h1), F32_SUBLANES), h1), :] = y
        return carry
    lax.fori_loop(0, n2, per_t2, 0, unroll=DFT_UNROLL)

    def per_t1(t1, carry):
        rows = pl.ds(pl.multiple_of(t1 * n2, n2), n2)
        y = y_ref[pl.ds(t1, n2, stride=_pitch(h1)), :]
        o_ref[0, rows, :] = (x0_ref[0, rows, :] * (y + xv_ref[0, rows, :] * bias_ref[...])).astype(BF16)
        return carry
    lax.fori_loop(0, h1, per_t1, 0, unroll=DFT_UNROLL)


def _longconv(x0, xv, kf, bias, f1, m2f, m2i, f3):
    b, s, hw = x0.shape
    n2 = DFT_N2
    n1 = 2 * s // n2
    cb = LANES
    tok = pl.BlockSpec((1, s, cb), lambda j, bb: (bb, 0, j))
    return pl.pallas_call(
        functools.partial(_longconv_kernel, n1=n1),
        grid=(hw // cb, b),
        in_specs=[tok, tok,
                  pl.BlockSpec((n1 // 2 + 1, 2 * n2, cb), lambda j, bb: (0, 0, j), pipeline_mode=pl.Buffered(1)),
                  pl.BlockSpec((1, cb), lambda j, bb: (0, j)),
                  _const_spec(f1.shape), _const_spec(m2f.shape), _const_spec(m2i.shape),
                  _const_spec(f3.shape)],
        out_specs=tok,
        out_shape=jax.ShapeDtypeStruct((b, s, hw), BF16),
        scratch_shapes=[pltpu.VMEM((n1 // 2 * _pitch(n2), cb), F32), pltpu.VMEM((n2 * _pitch(n1), cb), F32),
                        pltpu.VMEM((n1 // 2 * _pitch(2 * n2), cb), F32), pltpu.VMEM((n2 * _pitch(n1 // 2), cb), F32)],
        compiler_params=_cparams(2),
        name="longconv",
    )(x0, xv, kf, bias.reshape(1, hw), f1, m2f, m2i, f3)


NA_QROWS = NA_WIN_ROWS // 2
NA_BAND_ROWS = NA_QROWS + NA_WIN_ROWS


def _bias_kernel(rpb_ref, o_ref, *, n_dc):
    h = pl.program_id(0)
    n_dr = 2 * NA_WIN_ROWS - 1
    qcol = lax.broadcasted_iota(jnp.int32, (GRID_W, LANES), 0)
    lane = lax.broadcasted_iota(jnp.int32, (GRID_W, LANES), 1)
    kcol = lane % GRID_W
    win = jnp.clip(qcol - NA_WIN_COLS // 2, 0, GRID_W - NA_WIN_COLS)
    col_ok = (kcol >= win) & (kcol < win + NA_WIN_COLS)
    low_half = lane < GRID_W
    lane_row = lax.broadcasted_iota(jnp.int32, (1, LANES), 1)
    toeplitz = []
    for dr in range(n_dr):
        vec = jnp.zeros((1, LANES), F32)
        for d in range(n_dc):
            vec = jnp.where(lane_row == d, rpb_ref[(h * n_dr + dr) * n_dc + d], vec)
        rows = jnp.broadcast_to(vec * LOG2_E, (GRID_W, LANES))
        toeplitz.append([pltpu.roll(rows, (LANES - (NA_WIN_COLS - 1) + half * GRID_W) % LANES, 1,
                                    stride=1, stride_axis=0) for half in range(2)])
    neg = jnp.full((GRID_W, LANES), NEG_BIG, F32)
    for pos in range(3):
        for p in range(NA_QROWS):
            lo = (0, p, NA_QROWS)[pos]
            for jj in range(NA_BAND_ROWS // 2):
                halves = []
                for half in range(2):
                    j = 2 * jj + half
                    dr = j - p + NA_WIN_ROWS - 1 - NA_QROWS * pos
                    halves.append(toeplitz[dr][half] if lo <= j < lo + NA_WIN_ROWS else neg)
                tile = jnp.where(low_half, halves[0], halves[1])
                o_ref[pos, 0, p * GRID_W:(p + 1) * GRID_W, jj * LANES:(jj + 1) * LANES] = (
                    jnp.where(col_ok, tile, NEG_BIG))


def _bias_table(rpb):
    heads, n_dr, n_dc = rpb.shape
    blk = (NA_QROWS * GRID_W, NA_BAND_ROWS * GRID_W)
    assert 2 * GRID_W == LANES and NA_BAND_ROWS % 2 == 0 and n_dc <= LANES
    return pl.pallas_call(
        functools.partial(_bias_kernel, n_dc=n_dc),
        grid=(heads,),
        in_specs=[pl.BlockSpec(memory_space=pltpu.SMEM)],
        out_specs=pl.BlockSpec((3, 1) + blk, lambda h: (0, h, 0, 0)),
        out_shape=jax.ShapeDtypeStruct((3, heads) + blk, F32),
        compiler_params=_cparams(1),
        name="na_bias",
    )(rpb.reshape(-1))


def _nattn_kernel(q_ref, k_ref, v_ref, kc_ref, vc_ref, bias_ref, o_ref, vs_ref, vcs_ref, *, rows):
    dh = NA_HEAD_DIM
    nq = NA_QROWS * GRID_W
    band = NA_BAND_ROWS * GRID_W
    nt = (((1,), (1,)), ((), ()))

    def own_lanes(n_rows):
        lane = lax.broadcasted_iota(jnp.int32, (n_rows, LANES), 1)
        return [lane < dh, lane >= dh]

    for hh in range(2):
        vs_ref[hh] = jnp.where(own_lanes(v_ref.shape[1])[hh], v_ref[0], 1.0).astype(BF16)
        vcs_ref[hh] = jnp.where(own_lanes(vc_ref.shape[1])[hh], vc_ref[0], 1.0).astype(BF16)
    own_q = own_lanes(nq)
    n_blocks = rows // NA_QROWS

    def per_block(i, carry):
        r0 = i * NA_QROWS
        pos = jnp.where(i == 0, 0, jnp.where(i == n_blocks - 1, 2, 1))
        key0 = pl.multiple_of(jnp.clip(r0 - NA_WIN_ROWS // 2, 0, rows - NA_BAND_ROWS) * GRID_W, GRID_W)
        qrows = pl.ds(pl.multiple_of(r0 * GRID_W, nq), nq)
        q = q_ref[0, qrows, :]
        kb = k_ref[0, pl.ds(key0, band), :]
        outs = []
        for hh in range(2):
            qh = jnp.where(own_q[hh], q, 0.0).astype(BF16)
            s_loc = lax.dot_general(qh, kb, nt, preferred_element_type=F32) + bias_ref[pos, hh]
            s_ctx = lax.dot_general(qh, kc_ref[0], nt, preferred_element_type=F32)
            s = jnp.concatenate([s_loc, s_ctx], axis=-1)
            p = jnp.exp2(s - jnp.max(s, axis=-1, keepdims=True)).astype(BF16)
            o = _dot(p[:, 0:band], vs_ref[hh, pl.ds(key0, band), :]) + _dot(p[:, band:], vcs_ref[hh])
            outs.append(o / pltpu.roll(o, dh, axis=1))
        o_ref[0, qrows, :] = jnp.where(own_q[0], outs[0], outs[1]).astype(BF16)
        return carry
    lax.fori_loop(0, n_blocks, per_block, 0, unroll=NA_UNROLL)


def _nattn(q, k, v, kc, vc, bias):
    b, s, aw = q.shape
    nctx = kc.shape[1]
    hps = LANES // NA_HEAD_DIM
    assert (s // GRID_W) % NA_QROWS == 0 and s // GRID_W >= NA_BAND_ROWS
    tok = pl.BlockSpec((1, s, LANES), lambda j, bb: (bb, 0, j))
    ctx = pl.BlockSpec((1, nctx, LANES), lambda j, bb: (bb, 0, j))
    return pl.pallas_call(
        functools.partial(_nattn_kernel, rows=s // GRID_W),
        grid=(aw // LANES, b),
        in_specs=[tok, tok, tok, ctx, ctx,
                  pl.BlockSpec((3, hps) + bias.shape[2:], lambda j, bb: (0, j, 0, 0))],
        out_specs=tok,
        out_shape=jax.ShapeDtypeStruct((b, s, aw), BF16),
        scratch_shapes=[pltpu.VMEM((hps, s, LANES), BF16), pltpu.VMEM((hps, nctx, LANES), BF16)],
        compiler_params=_cparams(2),
        name="nattn",
    )(q, k, v, kc, vc, bias)


def _outproj_kernel(yh_ref, ya_ref, h_ref, mod_ref, g_ref, w_ref, o_ref, *, d, hw):
    gate = mod_ref[0, :, 2 * d:3 * d]
    y = _dot(yh_ref[0], w_ref[0:hw, :]) + _dot(ya_ref[0], w_ref[hw:, :])
    o_ref[0] = h_ref[0] + gate * _rms(y, g_ref[...])


def _outproj(y_hy, y_na, h, mod, g, w_out, tile):
    b, s, d = h.shape
    hw = y_hy.shape[2]
    aw = y_na.shape[2]
    tok = lambda width: pl.BlockSpec((1, tile, width), lambda bb, i: (bb, i, 0))
    return pl.pallas_call(
        functools.partial(_outproj_kernel, d=d, hw=hw),
        grid=(b, s // tile),
        in_specs=[tok(hw), tok(aw), tok(d),
                  pl.BlockSpec((1, 1, mod.shape[2]), lambda bb, i: (bb, 0, 0)),
                  _const_spec((1, d)), _const_spec(w_out.shape)],
        out_specs=tok(d),
        out_shape=jax.ShapeDtypeStruct((b, s, d), F32),
        compiler_params=_cparams(2),
        name="outproj",
    )(y_hy, y_na, h, mod, g.reshape(1, d), w_out)


def _ffn_kernel(prev_ref, main_ref, next_ref, mod_ref, gpre_ref, gpost_ref, wup_ref, cw_ref, cb_ref, wdn_ref,
                o_ref, hn_ref, z_ref, acc_ref, *, tile, d, n_chunks):
    fc = FFN_CHUNK
    shift = mod_ref[0, :, 3 * d:4 * d]
    scale = mod_ref[0, :, 4 * d:5 * d]
    gate = mod_ref[0, :, 5 * d:6 * d]
    g = gpre_ref[...]
    _fill_halo_tile(hn_ref, prev_ref, main_ref, next_ref,
                    lambda t: _mod_norm(t, g, scale, shift), tile)

    def chunk_start(j, base):
        start = base + j * fc
        return start if isinstance(j, int) else pl.multiple_of(start, LANES)

    def up(j, slot):
        z_ref[slot, :, 0:fc] = _dot(hn_ref[...], wup_ref[:, pl.ds(chunk_start(j, 0), fc)])
        z_ref[slot, :, fc:2 * fc] = _dot(hn_ref[...], wup_ref[:, pl.ds(chunk_start(j, n_chunks * fc), fc)])

    def down(j, slot):
        cw = cw_ref[j]
        c = (cw[0:1, :] * z_ref[slot, pl.ds(HALO - 1, tile), :]
             + cw[1:2, :] * z_ref[slot, pl.ds(HALO, tile), :]
             + cw[2:3, :] * z_ref[slot, pl.ds(HALO + 1, tile), :]
             + cb_ref[j])
        a = (_silu(c[:, 0:fc]) * c[:, fc:2 * fc]).astype(BF16)
        return _dot(a, wdn_ref[pl.ds(chunk_start(j, 0), fc), :])

    def step(j, slot):
        up(j + 1, 1 - slot)
        acc_ref[...] += down(j, slot)

    assert n_chunks >= 2
    up(0, 0)
    up(1, 1)
    acc_ref[...] = down(0, 0)
    n_steps = n_chunks - 2
    n_pairs = n_steps // 2

    def pair(jj, carry):
        step(2 * jj + 1, 1)
        step(2 * jj + 2, 0)
        return carry
    lax.fori_loop(0, n_pairs, pair, 0)
    if n_steps % 2:
        step(n_chunks - 2, (n_chunks - 2) % 2)
    y = acc_ref[...] + down(n_chunks - 1, (n_chunks - 1) % 2)
    o_ref[0] = main_ref[0] + gate * _rms(y, gpost_ref[...])


def _layer_spec(stacked, layer):
    nd = stacked.ndim - 1
    return pl.BlockSpec((None,) + stacked.shape[1:], lambda *_: (layer,) + (0,) * nd, pipeline_mode=pl.Buffered(1))


def _ffn(h, mod, g_pre, g_post, wup_all, w_dw, b_dw, wdn_all, layer, tile):
    b, s, d = h.shape
    dff = wdn_all.shape[1]
    fc = FFN_CHUNK
    nch = dff // fc
    pair = lambda t: jnp.concatenate([t[..., :dff].reshape(t.shape[:-1] + (nch, fc)),
                                      t[..., dff:].reshape(t.shape[:-1] + (nch, fc))], axis=-1)
    cw = jnp.transpose(pair(w_dw), (1, 0, 2))
    cb = pair(b_dw).reshape(nch, 1, 2 * fc)
    return pl.pallas_call(
        functools.partial(_ffn_kernel, tile=tile, d=d, n_chunks=nch),
        grid=(b, s // tile),
        in_specs=_halo_specs(tile, d, s) + [
            pl.BlockSpec((1, 1, mod.shape[2]), lambda bb, i: (bb, 0, 0)),
            _const_spec((1, d)), _const_spec((1, d)),
            _layer_spec(wup_all, layer), _const_spec(cw.shape), _const_spec(cb.shape), _layer_spec(wdn_all, layer)],
        out_specs=pl.BlockSpec((1, tile, d), lambda bb, i: (bb, i, 0)),
        out_shape=jax.ShapeDtypeStruct((b, s, d), F32),
        scratch_shapes=[pltpu.VMEM((tile + 2 * HALO, d), BF16),
                        pltpu.VMEM((2, tile + 2 * HALO, 2 * fc), F32),
                        pltpu.VMEM((tile, d), F32)],
        compiler_params=_cparams(2),
        name="convffn",
    )(h, h, h, mod, g_pre.reshape(1, d), g_post.reshape(1, d), wup_all, cw, cb, wdn_all)


CONF_TOKENS = 128


def _conformer_kernel(prev_ref, main_ref, next_ref, mod_ref, gpre_ref, gpost_ref, w1_ref, b1_ref,
                      dw_ref, db_ref, lg_ref, lb_ref, w2_ref, b2_ref, o_ref, hn_ref, u_ref, c_ref,
                      *, tile, d, seq):
    shift = mod_ref[0, :, 0:d]
    scale = mod_ref[0, :, d:2 * d]
    gate = mod_ref[0, :, 2 * d:3 * d]
    g = gpre_ref[...]
    _fill_halo_tile(hn_ref, prev_ref, main_ref, next_ref,
                    lambda t: _mod_norm(t, g, scale, shift), tile)
    ag = _dot(hn_ref[...], w1_ref[...]) + b1_ref[...]
    pos = (pl.program_id(1) * tile - HALO
           + lax.broadcasted_iota(jnp.int32, (tile + 2 * HALO, 1), 0))
    u = jnp.where((pos >= 0) & (pos < seq), ag[:, 0:d] * jax.nn.sigmoid(ag[:, d:2 * d]), 0.0)
    groups = d // LANES
    n_rows = tile + 2 * HALO
    for j in range(groups):
        u_ref[pl.ds(j, n_rows, stride=groups), :] = u[:, j * LANES:(j + 1) * LANES]
    first = HALO - CONF_K // 2

    def token_tile(ref, t):
        return ref[pl.ds(pl.multiple_of(t * groups, groups), groups), :]

    def conv_block(tb, carry):
        t0 = tb * CONF_TOKENS
        for i in range(CONF_TOKENS):
            acc = db_ref[...]
            for k in range(CONF_K):
                acc = acc + dw_ref[k] * token_tile(u_ref, t0 + i + first + k)
            c_ref[pl.ds(pl.multiple_of((t0 + i) * groups, groups), groups), :] = acc
        return carry
    lax.fori_loop(0, tile // CONF_TOKENS, conv_block, 0)
    c = jnp.concatenate([c_ref[pl.ds(j, tile, stride=groups), :] for j in range(groups)], axis=1)
    mu = jnp.mean(c, axis=-1, keepdims=True)
    cc = c - mu
    var = jnp.mean(cc * cc, axis=-1, keepdims=True)
    ln = cc * lax.rsqrt(var + EPS) * lg_ref[...] + lb_ref[...]
    y = _dot(_silu(ln).astype(BF16), w2_ref[...]) + b2_ref[...]
    o_ref[0] = main_ref[0] + gate * _rms(y, gpost_ref[...])


def _conformer(h, mod, g_pre, g_post, w_pw1, b_pw1, w_dw, b_dw, ln_g, ln_b, w_pw2, b_pw2, tile):
    b, s, d = h.shape
    groups = d // LANES
    assert groups == F32_SUBLANES and tile % CONF_TOKENS == 0
    row = lambda t: t.reshape(1, -1)
    return pl.pallas_call(
        functools.partial(_conformer_kernel, tile=tile, d=d, seq=s),
        grid=(b, s // tile),
        in_specs=_halo_specs(tile, d, s) + [
            pl.BlockSpec((1, 1, mod.shape[2]), lambda bb, i: (bb, 0, 0)),
            _const_spec((1, d)), _const_spec((1, d)),
            _const_spec(w_pw1.shape), _const_spec((1, 2 * d)),
            _const_spec((CONF_K, groups, LANES)), _const_spec((groups, LANES)),
            _const_spec((1, d)), _const_spec((1, d)),
            _const_spec(w_pw2.shape), _const_spec((1, d))],
        out_specs=pl.BlockSpec((1, tile, d), lambda bb, i: (bb, i, 0)),
        out_shape=jax.ShapeDtypeStruct((b, s, d), F32),
        scratch_shapes=[pltpu.VMEM((tile + 2 * HALO, d), BF16),
                        pltpu.VMEM(((tile + 2 * HALO) * groups, LANES), F32),
                        pltpu.VMEM((tile * groups, LANES), F32)],
        compiler_params=_cparams(2),
        name="conformer",
    )(h, h, h, mod, row(g_pre), row(g_post), w_pw1.astype(BF16), row(b_pw1),
      w_dw.reshape(CONF_K, groups, LANES), b_dw.reshape(groups, LANES),
      row(ln_g), row(ln_b), w_pw2.astype(BF16), row(b_pw2))


def _token_tile(seq):
    return min(seq, 1024)


def kernel(x, c, ctx, c_ctx, w_mod, b_mod, g_mix_pre, g_mix_post, g_ffn_pre, g_ffn_post, w_in, w_out, hy_short_w, hy_short_b, hy_f_w1, hy_f_b1, hy_f_w2, hy_f_b2, hy_f_w3, hy_f_b3, hy_f_w4, hy_f_freq, hy_bias, na_rpb, cf_w_pw1, cf_b_pw1, cf_w_dw, cf_b_dw, cf_ln_g, cf_ln_b, cf_w_pw2, cf_b_pw2, ffn_w_up, ffn_w_dw, ffn_b_dw, ffn_w_down):
    bsz, seq, d = x.shape
    depth = w_mod.shape[0]
    hw = hy_bias.shape[1]
    aw = NA_HEADS * NA_HEAD_DIM
    tile = _token_tile(seq)
    assert seq % tile == 0 and seq % (GRID_W * NA_WIN_ROWS) == 0 and tile % HALO == 0

    n_rows = -(-(bsz + 1) // BF16_SUBLANES) * BF16_SUBLANES
    cc = jnp.concatenate([c, c_ctx[None, :], jnp.zeros((n_rows - bsz - 1, d), F32)], axis=0)
    mod_all = _adaln(cc, w_mod, b_mod)
    ffn_wup, ffn_wdn = ffn_w_up.astype(BF16), ffn_w_down.astype(BF16)

    h = x
    for layer in range(depth):
        mod = mod_all[layer, :bsz].reshape(bsz, 1, 6 * d)
        if layer % 2 == 0:
            e = layer // 2
            w_e = w_in[e].astype(BF16)
            k_ctx, v_ctx = _ctxkv(ctx, mod_all[layer, bsz:bsz + 1, :2 * d], g_mix_pre[layer],
                                  w_e[:, 3 * hw + aw:])
            x0, xv, q, k, v = _inproj(h, mod, g_mix_pre[layer], w_e[:, :3 * hw], w_e[:, 3 * hw:],
                                      hy_short_w[e], hy_short_b[e], tile)
            f1, m2f, m2i, f3 = _dft_tables(seq)
            hf, hb = _hyena_filters(seq, hy_f_w1[e], hy_f_b1[e], hy_f_w2[e], hy_f_b2[e],
                                    hy_f_w3[e], hy_f_b3[e], hy_f_w4[e], hy_f_freq[e])
            kf = _filter_spectrum(hf, hb, f1, m2f)
            y_hy = _longconv(x0, xv, kf, hy_bias[e], f1, m2f, m2i, f3)
            y_na = _nattn(q, k, v, k_ctx, v_ctx, _bias_table(na_rpb[e]))
            h = _outproj(y_hy, y_na, h, mod, g_mix_post[layer], w_out[e].astype(BF16), tile)
        else:
            o = layer // 2
            h = _conformer(h, mod, g_mix_pre[layer], g_mix_post[layer], cf_w_pw1[o], cf_b_pw1[o],
                           cf_w_dw[o], cf_b_dw[o], cf_ln_g[o], cf_ln_b[o], cf_w_pw2[o], cf_b_pw2[o], tile)
        h = _ffn(h, mod, g_ffn_pre[layer], g_ffn_post[layer], ffn_wup, ffn_w_dw[layer],
                 ffn_b_dw[layer], ffn_wdn, layer, tile)
    return h
```
